```python
import math
import jax, jax.numpy as jnp
from jax import lax
import numpy as np

D_MODEL = 1024
BATCH = 16
SEQ = 2048
DEPTH = 1

PLE_DIM = 256
D_MIX = D_MODEL
D_SSM = D_MIX // 2
SSM_GROUP = 16
N_SSM_GROUPS = D_SSM // SSM_GROUP
SSM_STATE = 64
D_SGU = D_MIX - D_SSM
SGU_CHUNK = 128
N_SGU_HEADS = 4
SGU_HEAD_DIM = D_SGU // N_SGU_HEADS
D_IN = D_SSM + 2 * D_SGU
PEER_HEADS = 8
PEER_NKEYS = 128
PEER_NEXPERTS = PEER_NKEYS * PEER_NKEYS
PEER_TOPK = 16
PEER_HALF = 128
PEER_QDIM = 2 * PEER_HALF
PEER_BLOCK = 128
EPS = 1e-6

kernel_name = "hymba_s5_sgu_peer_block"


def _gelu(x):
    return jax.nn.gelu(x, approximate=False)


def _rmsnorm(x, g):
    x32 = x.astype(jnp.float32)
    y = x32 * lax.rsqrt(jnp.mean(x32 * x32, axis=-1, keepdims=True) + EPS)
    return (y * g.astype(jnp.float32)).astype(x.dtype)


def _s5(xs, a_re, a_im, log_dt, b_re, b_im, c_re, c_im, d):
    bsz, s, _ = xs.shape
    f32 = jnp.float32
    u = xs.astype(f32).reshape(bsz, s, N_SSM_GROUPS, SSM_GROUP)
    lam_re, lam_im = a_re.astype(f32), a_im.astype(f32)
    dt = jnp.exp(log_dt.astype(f32))[:, None]
    mag = jnp.exp(lam_re * dt)
    ang = lam_im * dt
    abar_re, abar_im = mag * jnp.cos(ang), mag * jnp.sin(ang)
    nr, ni = abar_re - 1.0, abar_im
    den = lam_re * lam_re + lam_im * lam_im
    coef_re = (nr * lam_re + ni * lam_im) / den
    coef_im = (ni * lam_re - nr * lam_im) / den
    br, bi = b_re.astype(f32), b_im.astype(f32)
    bbar_re = coef_re[..., None] * br - coef_im[..., None] * bi
    bbar_im = coef_re[..., None] * bi + coef_im[..., None] * br
    bu_re = jnp.einsum('bsgc,gpc->sbgp', u, bbar_re)
    bu_im = jnp.einsum('bsgc,gpc->sbgp', u, bbar_im)
    a_s_re = jnp.broadcast_to(abar_re[None, None], (s, 1, N_SSM_GROUPS, SSM_STATE))
    a_s_im = jnp.broadcast_to(abar_im[None, None], (s, 1, N_SSM_GROUPS, SSM_STATE))

    def combine(e1, e2):
        a1r, a1i, b1r, b1i = e1
        a2r, a2i, b2r, b2i = e2
        ar = a1r * a2r - a1i * a2i
        ai = a1r * a2i + a1i * a2r
        bo_r = a2r * b1r - a2i * b1i + b2r
        bo_i = a2r * b1i + a2i * b1r + b2i
        return ar, ai, bo_r, bo_i

    _, _, h_re, h_im = lax.associative_scan(combine, (a_s_re, a_s_im, bu_re, bu_im), axis=0)
    y = (jnp.einsum('sbgp,gcp->bsgc', h_re, c_re.astype(f32))
         - jnp.einsum('sbgp,gcp->bsgc', h_im, c_im.astype(f32)))
    y = y + d.astype(f32).reshape(N_SSM_GROUPS, SSM_GROUP) * u
    return y.reshape(bsz, s, D_SSM).astype(xs.dtype)


def _sgu(u, v, ln_g, ln_b, ws, bs):
    bsz, s, _ = u.shape
    nc = s // SGU_CHUNK
    u = _gelu(u)
    v = _gelu(v)
    v32 = v.astype(jnp.float32).reshape(bsz, s, N_SGU_HEADS, SGU_HEAD_DIM)
    mu = jnp.mean(v32, axis=-1, keepdims=True)
    var = jnp.mean(jnp.square(v32 - mu), axis=-1, keepdims=True)
    vn = ((v32 - mu) * lax.rsqrt(var + EPS)
          * ln_g.astype(jnp.float32).reshape(N_SGU_HEADS, SGU_HEAD_DIM)
          + ln_b.astype(jnp.float32).reshape(N_SGU_HEADS, SGU_HEAD_DIM)).astype(u.dtype)
    vc = vn.reshape(bsz, nc, SGU_CHUNK, N_SGU_HEADS, SGU_HEAD_DIM)
    causal = jnp.tril(jnp.ones((SGU_CHUNK, SGU_CHUNK), dtype=bool))
    wsm = jnp.where(causal[None], ws, jnp.zeros_like(ws))
    mixed = jnp.einsum('hts,bcshd->bcthd', wsm, vc) + bs.T[None, None, :, :, None]
    return u * mixed.reshape(bsz, s, D_SGU).astype(u.dtype)


def _peer(xn, wq, keys, u_tab, v_tab):
    bsz, s, d = xn.shape
    t = bsz * s
    xt = xn.reshape(t, d)
    q = (xt @ wq).reshape(t, PEER_HEADS, 2, PEER_HALF)
    sc = jnp.einsum('thkd,hknd->thkn', q, keys).astype(jnp.float32)
    s1, i1 = lax.top_k(sc[:, :, 0], PEER_TOPK)
    s2, i2 = lax.top_k(sc[:, :, 1], PEER_TOPK)
    cand = (s1[..., :, None] + s2[..., None, :]).reshape(t, PEER_HEADS, PEER_TOPK * PEER_TOPK)
    top_s, ci = lax.top_k(cand, PEER_TOPK)
    ia, ib = ci // PEER_TOPK, ci % PEER_TOPK
    experts = (jnp.take_along_axis(i1, ia, axis=-1) * PEER_NKEYS
               + jnp.take_along_axis(i2, ib, axis=-1))
    gates = jax.nn.softmax(top_s, axis=-1).astype(xn.dtype)
    nb = t // PEER_BLOCK

    def block(args):
        xb, eb, gb = args
        ub = u_tab[eb]
        act = _gelu(jnp.einsum('td,thkd->thk', xb, ub)) * gb
        vb = v_tab[eb]
        return jnp.einsum('thk,thkd->td', act, vb)

    y = lax.map(block, (xt.reshape(nb, PEER_BLOCK, d),
                        experts.reshape(nb, PEER_BLOCK, PEER_HEADS, PEER_TOPK),
                        gates.reshape(nb, PEER_BLOCK, PEER_HEADS, PEER_TOPK)))
    return y.reshape(bsz, s, d)


def setup_inputs(seed: int = 0) -> dict:
    key = jax.random.key(seed)
    ks = jax.random.split(key, 32)
    n = jax.random.normal
    L, D = DEPTH, D_MODEL
    G, P, C = N_SSM_GROUPS, SSM_STATE, SSM_GROUP
    a_im = jnp.broadcast_to(jnp.pi * jnp.arange(P, dtype=jnp.float32), (L, G, P))
    return {
        "x": n(ks[0], (BATCH, SEQ, D), jnp.float32),
        "p": n(ks[1], (DEPTH, BATCH, SEQ, PLE_DIM), jnp.float32),
        "ln_mix_g": 1.0 + 0.02 * n(ks[2], (L, D), jnp.float32),
        "w_in": n(ks[3], (L, D, D_IN), jnp.float32) * D ** -0.5,
        "s5_a_re": -0.5 + 0.01 * n(ks[4], (L, G, P), jnp.float32),
        "s5_a_im": a_im + 0.01 * n(ks[5], (L, G, P), jnp.float32),
        "s5_log_dt": jax.random.uniform(ks[6], (L, G), jnp.float32, math.log(1e-3), math.log(1e-1)),
        "s5_b_re": n(ks[7], (L, G, P, C), jnp.float32) * (2 * C) ** -0.5,
        "s5_b_im": n(ks[8], (L, G, P, C), jnp.float32) * (2 * C) ** -0.5,
        "s5_c_re": n(ks[9], (L, G, C, P), jnp.float32) * (2 * P) ** -0.5,
        "s5_c_im": n(ks[10], (L, G, C, P), jnp.float32) * (2 * P) ** -0.5,
        "s5_d": n(ks[11], (L, D_SSM), jnp.float32),
        "glu_w": n(ks[12], (L, D_SSM, D_SSM), jnp.float32) * D_SSM ** -0.5,
        "glu_b": 0.02 * n(ks[13], (L, D_SSM), jnp.float32),
        "sgu_ln_g": 1.0 + 0.02 * n(ks[14], (L, D_SGU), jnp.float32),
        "sgu_ln_b": 0.02 * n(ks[15], (L, D_SGU), jnp.float32),
        "sgu_ws": n(ks[16], (L, N_SGU_HEADS, SGU_CHUNK, SGU_CHUNK), jnp.float32) * SGU_CHUNK ** -0.5,
        "sgu_b": 1.0 + 0.02 * n(ks[17], (L, N_SGU_HEADS, SGU_CHUNK), jnp.float32),
        "out_norm_ssm": 1.0 + 0.02 * n(ks[18], (L, D_SSM), jnp.float32),
        "out_norm_sgu": 1.0 + 0.02 * n(ks[19], (L, D_SGU), jnp.float32),
        "w_out": n(ks[20], (L, D_MIX, D), jnp.float32) * D_MIX ** -0.5,
        "ln_ffn_g": 1.0 + 0.02 * n(ks[21], (L, D), jnp.float32),
        "peer_wq": n(ks[22], (L, D, PEER_HEADS * PEER_QDIM), jnp.float32) * D ** -0.5,
        "peer_keys": n(ks[23], (L, PEER_HEADS, 2, PEER_NKEYS, PEER_HALF), jnp.float32) * PEER_HALF ** -0.5,
        "peer_u": n(ks[24], (L, PEER_NEXPERTS, D), jnp.float32) * D ** -0.5,
        "peer_v": n(ks[25], (L, PEER_NEXPERTS, D), jnp.float32) * PEER_HEADS ** -0.5,
        "ple_norm_g": 1.0 + 0.02 * n(ks[26], (L, D), jnp.float32),
        "ple_w_gate": n(ks[27], (L, D, D), jnp.float32) * D ** -0.5,
        "ple_w_proj": n(ks[28], (L, PLE_DIM, D), jnp.float32) * PLE_DIM ** -0.5,
        "final_g": 1.0 + 0.02 * n(ks[29], (D,), jnp.float32),
    }


def reference(x, p, ln_mix_g, w_in, s5_a_re, s5_a_im, s5_log_dt, s5_b_re, s5_b_im,
              s5_c_re, s5_c_im, s5_d, glu_w, glu_b, sgu_ln_g, sgu_ln_b, sgu_ws, sgu_b,
              out_norm_ssm, out_norm_sgu, w_out, ln_ffn_g, peer_wq, peer_keys, peer_u,
              peer_v, ple_norm_g, ple_w_gate, ple_w_proj, final_g):
    h = x
    for i in range(DEPTH):
        xn = _rmsnorm(h, ln_mix_g[i])
        z = xn @ w_in[i]
        xs = z[..., :D_SSM]
        u = z[..., D_SSM:D_SSM + D_SGU]
        v = z[..., D_SSM + D_SGU:]
        ys = _s5(xs, s5_a_re[i], s5_a_im[i], s5_log_dt[i], s5_b_re[i], s5_b_im[i],
                 s5_c_re[i], s5_c_im[i], s5_d[i])
        ys = _gelu(ys)
        ys = ys * jax.nn.sigmoid(ys @ glu_w[i] + glu_b[i])
        yg = _sgu(u, v, sgu_ln_g[i], sgu_ln_b[i], sgu_ws[i], sgu_b[i])
        mix = jnp.concatenate([_rmsnorm(ys, out_norm_ssm[i]),
                               _rmsnorm(yg, out_norm_sgu[i])], axis=-1)
        h = h + mix @ w_out[i]
        hn = _rmsnorm(h, ln_ffn_g[i])
        h = h + _peer(hn, peer_wq[i], peer_keys[i], peer_u[i], peer_v[i])
        gate = jax.nn.sigmoid(_rmsnorm(h, ple_norm_g[i]) @ ple_w_gate[i])
        h = h + gate * (p[i] @ ple_w_proj[i])
    return _rmsnorm(h, final_g)
```

```python
import functools
import math

import jax
import jax.numpy as jnp
from jax import lax
from jax.experimental import pallas as pl
from jax.experimental.pallas import tpu as pltpu

F32 = jnp.float32
BF16 = jnp.bfloat16
EPS = 1e-6
LANES = 128
SUBLANES = 8
VMEM_LIMIT = 56 * 1024 * 1024


def _gelu(x):
    return 0.5 * x * (1.0 + lax.erf(x * (1.0 / math.sqrt(2.0))))


def _rms(x, g):
    return x * lax.rsqrt(jnp.mean(x * x, axis=-1, keepdims=True) + EPS) * g


def _const_spec(shape):
    nd = len(shape)
    return pl.BlockSpec(shape, lambda *_: (0,) * nd)


def _mix_in_kernel(x_ref, g_ref, w_ref, xs_ref, u_ref, v_ref):
    xn = _rms(x_ref[...], g_ref[...])
    z = jnp.dot(xn.astype(BF16), w_ref[...], preferred_element_type=F32)
    d_ssm = xs_ref.shape[-1]
    d_sgu = u_ref.shape[-1]
    xs_ref[...] = z[:, :d_ssm]
    u_ref[...] = z[:, d_ssm:d_ssm + d_sgu]
    v_ref[...] = z[:, d_ssm + d_sgu:]


def _mix_in(x2, g, w_in, d_ssm, d_sgu, tm):
    t, d = x2.shape
    row = lambda i: (i, 0)
    return pl.pallas_call(
        _mix_in_kernel,
        grid=(t // tm,),
        in_specs=[pl.BlockSpec((tm, d), row), _const_spec((1, d)), _const_spec(w_in.shape)],
        out_specs=[pl.BlockSpec((tm, d_ssm), row), pl.BlockSpec((tm, d_sgu), row),
                   pl.BlockSpec((tm, d_sgu), row)],
        out_shape=[jax.ShapeDtypeStruct((t, d_ssm), F32), jax.ShapeDtypeStruct((t, d_sgu), F32),
                   jax.ShapeDtypeStruct((t, d_sgu), F32)],
        compiler_params=pltpu.CompilerParams(dimension_semantics=("parallel",),
                                             vmem_limit_bytes=VMEM_LIMIT),
        name="mix_in",
    )(x2, g, w_in)


def _s5_kernel(xs_ref, bbd_ref, cbd_ref, are_ref, aim_ref, d_ref, y_ref, bu_ref, hs_ref, h_ref,
               *, seq_blk, n_state, lane_chunk):
    @pl.when(pl.program_id(1) == 0)
    def _():
        h_ref[...] = jnp.zeros_like(h_ref)

    nb = xs_ref.shape[0]
    x = xs_ref[...].reshape(nb * seq_blk, xs_ref.shape[-1])
    bu = jnp.dot(x.astype(BF16), bbd_ref[...], preferred_element_type=F32)
    n_tiles = 2 * n_state // LANES
    for j in range(n_tiles):
        bu_ref[j] = bu[:, j * LANES:(j + 1) * LANES]

    tiles = lane_chunk // LANES
    for c in range(n_state // lane_chunk):
        re = [c * tiles + j for j in range(tiles)]
        im = [n_state // LANES + j for j in re]
        lanes = pl.ds(c * lane_chunk, lane_chunk)
        ar = jnp.broadcast_to(are_ref[:, lanes], (nb, lane_chunk))
        ai = jnp.broadcast_to(aim_ref[:, lanes], (nb, lane_chunk))

        def step(t, carry, re=re, im=im, ar=ar, ai=ai):
            hr, hi = carry
            rows = pl.ds(t, nb, stride=seq_blk)
            br = jnp.concatenate([bu_ref[j, rows, :] for j in re], axis=1)
            bi = jnp.concatenate([bu_ref[j, rows, :] for j in im], axis=1)
            nr = ar * hr - ai * hi + br
            ni = ar * hi + ai * hr + bi
            for k, j in enumerate(re):
                hs_ref[j, rows, :] = nr[:, k * LANES:(k + 1) * LANES]
            for k, j in enumerate(im):
                hs_ref[j, rows, :] = ni[:, k * LANES:(k + 1) * LANES]
            return nr, ni

        im_lanes = pl.ds(n_state + c * lane_chunk, lane_chunk)
        hr, hi = lax.fori_loop(0, seq_blk, step, (h_ref[:, lanes], h_ref[:, im_lanes]))
        h_ref[:, lanes] = hr
        h_ref[:, im_lanes] = hi

    hs = jnp.concatenate([hs_ref[j] for j in range(n_tiles)], axis=1)
    y = jnp.dot(hs.astype(BF16), cbd_ref[...], preferred_element_type=F32)
    y = y + d_ref[...] * x
    y_ref[...] = y.reshape(y_ref.shape)


def _s5(xs3, bbd, cbd, a_re, a_im, d_skip, nb, seq_blk):
    b, s, d_ssm = xs3.shape
    n_state = a_re.shape[-1]
    blk = pl.BlockSpec((nb, seq_blk, d_ssm), lambda i, j: (i, j, 0))
    kern = functools.partial(_s5_kernel, seq_blk=seq_blk, n_state=n_state,
                             lane_chunk=min(n_state, 8 * LANES))
    return pl.pallas_call(
        kern,
        grid=(b // nb, s // seq_blk),
        in_specs=[blk, _const_spec(bbd.shape), _const_spec(cbd.shape), _const_spec(a_re.shape),
                  _const_spec(a_im.shape), _const_spec(d_skip.shape)],
        out_specs=blk,
        out_shape=jax.ShapeDtypeStruct(xs3.shape, F32),
        scratch_shapes=[pltpu.VMEM((2 * n_state // LANES, nb * seq_blk, LANES), F32),
                        pltpu.VMEM((2 * n_state // LANES, nb * seq_blk, LANES), F32),
                        pltpu.VMEM((nb, 2 * n_state), F32)],
        compiler_params=pltpu.CompilerParams(dimension_semantics=("parallel", "arbitrary"),
                                             vmem_limit_bytes=VMEM_LIMIT),
        name="s5_scan",
    )(xs3, bbd, cbd, a_re, a_im, d_skip)


def _mix_out_kernel(x_ref, y_ref, u_ref, v_ref, gluw_ref, glub_ref, lng_ref, lnb_ref, ws_ref,
                    bst_ref, ons_ref, ong_ref, wout_ref, lnf_ref, h1_ref, hn_ref, *, chunk):
    tm = x_ref.shape[0]
    n_heads = ws_ref.shape[0]
    hd = u_ref.shape[-1] // n_heads

    ys = _gelu(y_ref[...])
    gl = jnp.dot(ys.astype(BF16), gluw_ref[...], preferred_element_type=F32) + glub_ref[...]
    ys = ys * jax.nn.sigmoid(gl)
    ysn = _rms(ys, ons_ref[...])

    ug = _gelu(u_ref[...])
    vg = _gelu(v_ref[...])
    row = lax.broadcasted_iota(jnp.int32, (chunk, chunk), 0)
    col = lax.broadcasted_iota(jnp.int32, (chunk, chunk), 1)
    causal = row >= col
    heads = []
    for h in range(n_heads):
        cols = slice(h * hd, (h + 1) * hd)
        vh = vg[:, cols]
        mu = jnp.mean(vh, axis=-1, keepdims=True)
        var = jnp.mean(jnp.square(vh - mu), axis=-1, keepdims=True)
        vn = ((vh - mu) * lax.rsqrt(var + EPS) * lng_ref[:, cols] + lnb_ref[:, cols]).astype(BF16)
        w = jnp.where(causal, ws_ref[h], 0.0).astype(BF16)
        bias = bst_ref[:, h:h + 1]
        parts = [jnp.dot(w, vn[c * chunk:(c + 1) * chunk, :], preferred_element_type=F32) + bias
                 for c in range(tm // chunk)]
        heads.append(jnp.concatenate(parts, axis=0))
    yg = ug * jnp.concatenate(heads, axis=1)
    ygn = _rms(yg, ong_ref[...])

    mix = jnp.concatenate([ysn, ygn], axis=1).astype(BF16)
    h1 = x_ref[...] + jnp.dot(mix, wout_ref[...], preferred_element_type=F32)
    h1_ref[...] = h1
    hn_ref[...] = _rms(h1, lnf_ref[...])


def _mix_out(x2, y2, u2, v2, glu_w, glu_b, ln_g, ln_b, ws, bst, ons, ong, w_out, lnf, tm, chunk):
    t, d = x2.shape
    row = lambda i: (i, 0)
    rows = lambda a: pl.BlockSpec((tm, a.shape[-1]), row)
    consts = [glu_w, glu_b, ln_g, ln_b, ws, bst, ons, ong, w_out, lnf]
    return pl.pallas_call(
        functools.partial(_mix_out_kernel, chunk=chunk),
        grid=(t // tm,),
        in_specs=[rows(x2), rows(y2), rows(u2), rows(v2)] + [_const_spec(c.shape) for c in consts],
        out_specs=[pl.BlockSpec((tm, d), row), pl.BlockSpec((tm, d), row)],
        out_shape=[jax.ShapeDtypeStruct((t, d), F32), jax.ShapeDtypeStruct((t, d), F32)],
        compiler_params=pltpu.CompilerParams(dimension_semantics=("parallel",),
                                             vmem_limit_bytes=VMEM_LIMIT),
        name="mix_out",
    )(x2, y2, u2, v2, *consts)


def _topk_rows(sc, k):
    n = sc.shape[0]
    iota = lax.broadcasted_iota(jnp.int32, sc.shape, 0)
    vals, idxs = [], []
    for _ in range(k):
        m = jnp.max(sc, axis=0, keepdims=True)
        i = jnp.min(jnp.where(sc == m, iota, n), axis=0, keepdims=True)
        vals.append(m)
        idxs.append(i)
        sc = jnp.where(iota == i, -jnp.inf, sc)
    return jnp.concatenate(vals, axis=0), jnp.concatenate(idxs, axis=0)


def _router_kernel(hn_ref, wq_ref, keys_ref, e_ref, g_ref, *, topk):
    n_heads, _, n_keys, half = keys_ref.shape
    q = jnp.dot(hn_ref[...].astype(BF16), wq_ref[...], preferred_element_type=F32)
    nt = (((1,), (1,)), ((), ()))
    for h in range(n_heads):
        tops = []
        for k in range(2):
            qhk = q[:, (2 * h + k) * half:(2 * h + k + 1) * half].astype(BF16)
            sct = lax.dot_general(keys_ref[h, k], qhk, nt, preferred_element_type=F32)
            tops.append(_topk_rows(sct, topk))
        (s1, i1), (s2, i2) = tops
        cand = jnp.concatenate([s1[a:a + 1] + s2 for a in range(topk)], axis=0)
        ts, ci = _topk_rows(cand, topk)
        ia = ci // topk
        ib = ci % topk
        e1 = jnp.zeros_like(ci)
        e2 = jnp.zeros_like(ci)
        for a in range(topk):
            e1 = jnp.where(ia == a, i1[a:a + 1], e1)
            e2 = jnp.where(ib == a, i2[a:a + 1], e2)
        p = jnp.exp(ts - ts[0:1])
        rows = slice(h * topk, (h + 1) * topk)
        e_ref[rows, :] = e1 * n_keys + e2
        g_ref[rows, :] = p / jnp.sum(p, axis=0, keepdims=True)


def _router(hn2, wq, keys, topk, tm):
    t, d = hn2.shape
    n_heads = keys.shape[0]
    out_blk = pl.BlockSpec((n_heads * topk, tm), lambda i: (0, i))
    return pl.pallas_call(
        functools.partial(_router_kernel, topk=topk),
        grid=(t // tm,),
        in_specs=[pl.BlockSpec((tm, d), lambda i: (i, 0)), _const_spec(wq.shape),
                  _const_spec(keys.shape)],
        out_specs=[out_blk, out_blk],
        out_shape=[jax.ShapeDtypeStruct((n_heads * topk, t), jnp.int32),
                   jax.ShapeDtypeStruct((n_heads * topk, t), F32)],
        compiler_params=pltpu.CompilerParams(dimension_semantics=("parallel",),
                                             vmem_limit_bytes=VMEM_LIMIT),
        name="router",
    )(hn2, wq, keys)


def _peer_kernel(idx_ref, hn_ref, g_ref, h1_ref, tab_ref, o_ref, buf_ref, sem_ref, *, sub):
    tb, n_sel = idx_ref.shape
    n_sub = tb // sub
    rows_per_sub = sub * n_sel

    def issue(sb, slot):
        def body(i, carry):
            t = i // n_sel
            k = i % n_sel
            e = idx_ref[sb * sub + t, k]
            pltpu.make_async_copy(tab_ref.at[pl.ds(e, 1)], buf_ref.at[slot, pl.ds(i, 1)],
                                  sem_ref.at[slot]).start()
            return carry
        lax.fori_loop(0, rows_per_sub, body, 0)

    def wait(slot):
        pltpu.make_async_copy(tab_ref.at[pl.ds(0, rows_per_sub)], buf_ref.at[slot],
                              sem_ref.at[slot]).wait()

    def compute(sb, slot):
        base = pl.multiple_of(sb * sub, sub)
        gt = g_ref[pl.ds(base, sub), :].T
        xs = hn_ref[pl.ds(base, sub), :]
        ys = []
        for t in range(sub):
            w = buf_ref[slot, pl.ds(t * n_sel, n_sel), :]
            u = pltpu.bitcast(w & jnp.uint32(0xFFFF0000), F32)
            v = pltpu.bitcast(w << 16, F32)
            s = jnp.sum(u * xs[t:t + 1, :], axis=1, keepdims=True)
            act = _gelu(s) * gt[:, t:t + 1]
            ys.append(jnp.sum(act * v, axis=0, keepdims=True))
        y = jnp.concatenate(ys, axis=0)
        o_ref[pl.ds(base, sub), :] = h1_ref[pl.ds(base, sub), :] + y

    issue(0, 0)

    def outer(sb, carry):
        slot = sb % 2

        @pl.when(sb + 1 < n_sub)
        def _():
            issue(sb + 1, 1 - slot)

        wait(slot)
        compute(sb, slot)
        return carry

    lax.fori_loop(0, n_sub, outer, 0)


def _peer(experts, hn2, gates, h1, table, tb, sub):
    t, d = hn2.shape
    n_sel = experts.shape[-1]
    row = lambda i: (i, 0)
    return pl.pallas_call(
        functools.partial(_peer_kernel, sub=sub),
        grid=(t // tb,),
        in_specs=[pl.BlockSpec((tb, n_sel), row, memory_space=pltpu.SMEM),
                  pl.BlockSpec((tb, d), row), pl.BlockSpec((tb, n_sel), row),
                  pl.BlockSpec((tb, d), row), pl.BlockSpec(memory_space=pl.ANY)],
        out_specs=pl.BlockSpec((tb, d), row),
        out_shape=jax.ShapeDtypeStruct((t, d), F32),
        scratch_shapes=[pltpu.VMEM((2, sub * n_sel, d), jnp.uint32),
                        pltpu.SemaphoreType.DMA((2,))],
        compiler_params=pltpu.CompilerParams(dimension_semantics=("parallel",),
                                             vmem_limit_bytes=VMEM_LIMIT),
        name="peer",
    )(experts, hn2, gates, h1, table)


def _ple_kernel(h_ref, p_ref, ng_ref, wg_ref, wp_ref, fg_ref, o_ref):
    h = h_ref[...]
    hn = _rms(h, ng_ref[...])
    gate = jax.nn.sigmoid(jnp.dot(hn.astype(BF16), wg_ref[...], preferred_element_type=F32))
    proj = jnp.dot(p_ref[...].astype(BF16), wp_ref[...], preferred_element_type=F32)
    o_ref[...] = _rms(h + gate * proj, fg_ref[...])


def _ple(h2, p2, ng, wg, wp, fg, tm):
    t, d = h2.shape
    row = lambda i: (i, 0)
    return pl.pallas_call(
        _ple_kernel,
        grid=(t // tm,),
        in_specs=[pl.BlockSpec((tm, d), row), pl.BlockSpec((tm, p2.shape[-1]), row),
                  _const_spec(ng.shape), _const_spec(wg.shape), _const_spec(wp.shape),
                  _const_spec(fg.shape)],
        out_specs=pl.BlockSpec((tm, d), row),
        out_shape=jax.ShapeDtypeStruct((t, d), F32),
        compiler_params=pltpu.CompilerParams(dimension_semantics=("parallel",),
                                             vmem_limit_bytes=VMEM_LIMIT),
        name="ple_final",
    )(h2, p2, ng, wg, wp, fg)


def _s5_params(a_re, a_im, log_dt, b_re, b_im, c_re, c_im):
    g, p = a_re.shape
    c = b_re.shape[-1]
    dt = jnp.exp(log_dt)[:, None]
    mag = jnp.exp(a_re * dt)
    ang = a_im * dt
    abar_re, abar_im = mag * jnp.cos(ang), mag * jnp.sin(ang)
    nr, ni = abar_re - 1.0, abar_im
    den = a_re * a_re + a_im * a_im
    coef_re = (nr * a_re + ni * a_im) / den
    coef_im = (ni * a_re - nr * a_im) / den
    bbar_re = coef_re[..., None] * b_re - coef_im[..., None] * b_im
    bbar_im = coef_re[..., None] * b_im + coef_im[..., None] * b_re
    eye = jnp.eye(g, dtype=F32)
    blockdiag_in = lambda m: jnp.einsum('gpc,gh->gchp', m, eye).reshape(g * c, g * p)
    blockdiag_out = lambda m: jnp.einsum('gcp,gh->gphc', m, eye).reshape(g * p, g * c)
    bbd = jnp.concatenate([blockdiag_in(bbar_re), blockdiag_in(bbar_im)], axis=1)
    cbd = jnp.concatenate([blockdiag_out(c_re), blockdiag_out(-c_im)], axis=0)
    return (bbd.astype(BF16), cbd.astype(BF16), abar_re.reshape(1, g * p), abar_im.reshape(1, g * p))


def _pack_tables(u_tab, v_tab):
    ub = lax.bitcast_convert_type(u_tab.astype(BF16), jnp.uint16).astype(jnp.uint32)
    vb = lax.bitcast_convert_type(v_tab.astype(BF16), jnp.uint16).astype(jnp.uint32)
    return (ub << 16) | vb


def _pick(n, pref):
    while n % pref:
        pref //= 2
    return pref


def kernel(x, p, ln_mix_g, w_in, s5_a_re, s5_a_im, s5_log_dt, s5_b_re, s5_b_im, s5_c_re, s5_c_im, s5_d, glu_w, glu_b, sgu_ln_g, sgu_ln_b, sgu_ws, sgu_b, out_norm_ssm, out_norm_sgu, w_out, ln_ffn_g, peer_wq, peer_keys, peer_u, peer_v, ple_norm_g, ple_w_gate, ple_w_proj, final_g):
    bsz, seq, d = x.shape
    t = bsz * seq
    depth = w_in.shape[0]
    d_ssm = glu_w.shape[-1]
    d_sgu = sgu_ln_g.shape[-1]
    chunk = sgu_ws.shape[-1]
    topk = 16
    r1 = lambda a: a.reshape(1, -1)

    assert depth == 1, "the final rmsnorm is fused into the last layer's ple kernel"
    h = x.reshape(t, d)
    for i in range(depth):
        xs, u, v = _mix_in(h, r1(ln_mix_g[i]), w_in[i].astype(BF16), d_ssm, d_sgu, _pick(t, 512))
        bbd, cbd, a_re, a_im = _s5_params(s5_a_re[i], s5_a_im[i], s5_log_dt[i], s5_b_re[i],
                                          s5_b_im[i], s5_c_re[i], s5_c_im[i])
        y = _s5(xs.reshape(bsz, seq, d_ssm), bbd, cbd, a_re, a_im, r1(s5_d[i]),
                _pick(bsz, SUBLANES), _pick(seq, 64))
        h1, hn = _mix_out(h, y.reshape(t, d_ssm), u, v, glu_w[i].astype(BF16), r1(glu_b[i]),
                          r1(sgu_ln_g[i]), r1(sgu_ln_b[i]), sgu_ws[i], sgu_b[i].T,
                          r1(out_norm_ssm[i]), r1(out_norm_sgu[i]), w_out[i].astype(BF16),
                          r1(ln_ffn_g[i]), _pick(t, 2 * chunk), chunk)
        e_t, g_t = _router(hn, peer_wq[i].astype(BF16), peer_keys[i].astype(BF16), topk,
                           _pick(t, 256))
        h2 = _peer(e_t.T, hn, g_t.T, h1, _pack_tables(peer_u[i], peer_v[i]), _pick(t, 64), 8)
        h = _ple(h2, p[i].reshape(t, -1), r1(ple_norm_g[i]), ple_w_gate[i].astype(BF16),
                 ple_w_proj[i].astype(BF16), r1(final_g), _pick(t, 512))
    return h.reshape(bsz, seq, d)
```

```python
import functools
import math

import jax
import jax.numpy as jnp
from jax import lax
from jax.experimental import pallas as pl
from jax.experimental.pallas import tpu as pltpu

F32 = jnp.float32
BF16 = jnp.bfloat16
EPS = 1e-6
LANES = 128
SUBLANES = 8
VMEM_LIMIT = 56 * 1024 * 1024


def _gelu(x):
    return 0.5 * x * (1.0 + lax.erf(x * (1.0 / math.sqrt(2.0))))


def _rms(x, g):
    return x * lax.rsqrt(jnp.mean(x * x, axis=-1, keepdims=True) + EPS) * g


def _const_spec(shape):
    nd = len(shape)
    return pl.BlockSpec(shape, lambda *_: (0,) * nd)


def _mix_in_kernel(x_ref, g_ref, w_ref, xs_ref, u_ref, v_ref):
    xn = _rms(x_ref[...], g_ref[...])
    z = jnp.dot(xn.astype(BF16), w_ref[...], preferred_element_type=F32)
    d_ssm = xs_ref.shape[-1]
    d_sgu = u_ref.shape[-1]
    xs_ref[...] = z[:, :d_ssm]
    u_ref[...] = z[:, d_ssm:d_ssm + d_sgu]
    v_ref[...] = z[:, d_ssm + d_sgu:]


def _mix_in(x2, g, w_in, d_ssm, d_sgu, tm):
    t, d = x2.shape
    row = lambda i: (i, 0)
    return pl.pallas_call(
        _mix_in_kernel,
        grid=(t // tm,),
        in_specs=[pl.BlockSpec((tm, d), row), _const_spec((1, d)), _const_spec(w_in.shape)],
        out_specs=[pl.BlockSpec((tm, d_ssm), row), pl.BlockSpec((tm, d_sgu), row),
                   pl.BlockSpec((tm, d_sgu), row)],
        out_shape=[jax.ShapeDtypeStruct((t, d_ssm), F32), jax.ShapeDtypeStruct((t, d_sgu), F32),
                   jax.ShapeDtypeStruct((t, d_sgu), F32)],
        compiler_params=pltpu.CompilerParams(dimension_semantics=("parallel",),
                                             vmem_limit_bytes=VMEM_LIMIT),
        name="mix_in",
    )(x2, g, w_in)


def _s5_kernel(xs_ref, bbd_ref, cbd_ref, are_ref, aim_ref, d_ref, y_ref, bu_ref, hs_ref, h_ref,
               *, seq_blk, n_state, lane_chunk):
    @pl.when(pl.program_id(1) == 0)
    def _():
        h_ref[...] = jnp.zeros_like(h_ref)

    nb = xs_ref.shape[0]
    x = xs_ref[...].reshape(nb * seq_blk, xs_ref.shape[-1])
    bu = jnp.dot(x.astype(BF16), bbd_ref[...], preferred_element_type=F32)
    n_tiles = 2 * n_state // LANES
    for j in range(n_tiles):
        bu_ref[j] = bu[:, j * LANES:(j + 1) * LANES]

    tiles = lane_chunk // LANES
    for c in range(n_state // lane_chunk):
        re = [c * tiles + j for j in range(tiles)]
        im = [n_state // LANES + j for j in re]
        lanes = pl.ds(c * lane_chunk, lane_chunk)
        ar = jnp.broadcast_to(are_ref[:, lanes], (nb, lane_chunk))
        ai = jnp.broadcast_to(aim_ref[:, lanes], (nb, lane_chunk))

        def step(t, carry, re=re, im=im, ar=ar, ai=ai):
            hr, hi = carry
            rows = pl.ds(t, nb, stride=seq_blk)
            br = jnp.concatenate([bu_ref[j, rows, :] for j in re], axis=1)
            bi = jnp.concatenate([bu_ref[j, rows, :] for j in im], axis=1)
            nr = ar * hr - ai * hi + br
            ni = ar * hi + ai * hr + bi
            for k, j in enumerate(re):
                hs_ref[j, rows, :] = nr[:, k * LANES:(k + 1) * LANES]
            for k, j in enumerate(im):
                hs_ref[j, rows, :] = ni[:, k * LANES:(k + 1) * LANES]
            return nr, ni

        im_lanes = pl.ds(n_state + c * lane_chunk, lane_chunk)
        hr, hi = lax.fori_loop(0, seq_blk, step, (h_ref[:, lanes], h_ref[:, im_lanes]))
        h_ref[:, lanes] = hr
        h_ref[:, im_lanes] = hi

    hs = jnp.concatenate([hs_ref[j] for j in range(n_tiles)], axis=1)
    y = jnp.dot(hs.astype(BF16), cbd_ref[...], preferred_element_type=F32)
    y = y + d_ref[...] * x
    y_ref[...] = y.reshape(y_ref.shape)


def _s5(xs3, bbd, cbd, a_re, a_im, d_skip, nb, seq_blk):
    b, s, d_ssm = xs3.shape
    n_state = a_re.shape[-1]
    blk = pl.BlockSpec((nb, seq_blk, d_ssm), lambda i, j: (i, j, 0))
    kern = functools.partial(_s5_kernel, seq_blk=seq_blk, n_state=n_state,
                             lane_chunk=min(n_state, 8 * LANES))
    return pl.pallas_call(
        kern,
        grid=(b // nb, s // seq_blk),
        in_specs=[blk, _const_spec(bbd.shape), _const_spec(cbd.shape), _const_spec(a_re.shape),
                  _const_spec(a_im.shape), _const_spec(d_skip.shape)],
        out_specs=blk,
        out_shape=jax.ShapeDtypeStruct(xs3.shape, F32),
        scratch_shapes=[pltpu.VMEM((2 * n_state // LANES, nb * seq_blk, LANES), F32),
                        pltpu.VMEM((2 * n_state // LANES, nb * seq_blk, LANES), F32),
                        pltpu.VMEM((nb, 2 * n_state), F32)],
        compiler_params=pltpu.CompilerParams(dimension_semantics=("parallel", "arbitrary"),
                                             vmem_limit_bytes=VMEM_LIMIT),
        name="s5_scan",
    )(xs3, bbd, cbd, a_re, a_im, d_skip)


def _mix_out_kernel(x_ref, y_ref, u_ref, v_ref, gluw_ref, glub_ref, lng_ref, lnb_ref, ws_ref,
                    bst_ref, ons_ref, ong_ref, wout_ref, lnf_ref, h1_ref, hn_ref, *, chunk):
    tm = x_ref.shape[0]
    n_heads = ws_ref.shape[0]
    hd = u_ref.shape[-1] // n_heads

    ys = _gelu(y_ref[...])
    gl = jnp.dot(ys.astype(BF16), gluw_ref[...], preferred_element_type=F32) + glub_ref[...]
    ys = ys * jax.nn.sigmoid(gl)
    ysn = _rms(ys, ons_ref[...])

    ug = _gelu(u_ref[...])
    vg = _gelu(v_ref[...])
    row = lax.broadcasted_iota(jnp.int32, (chunk, chunk), 0)
    col = lax.broadcasted_iota(jnp.int32, (chunk, chunk), 1)
    causal = row >= col
    heads = []
    for h in range(n_heads):
        cols = slice(h * hd, (h + 1) * hd)
        vh = vg[:, cols]
        mu = jnp.mean(vh, axis=-1, keepdims=True)
        var = jnp.mean(jnp.square(vh - mu), axis=-1, keepdims=True)
        vn = ((vh - mu) * lax.rsqrt(var + EPS) * lng_ref[:, cols] + lnb_ref[:, cols]).astype(BF16)
        w = jnp.where(causal, ws_ref[h], 0.0).astype(BF16)
        bias = bst_ref[:, h:h + 1]
        parts = [jnp.dot(w, vn[c * chunk:(c + 1) * chunk, :], preferred_element_type=F32) + bias
                 for c in range(tm // chunk)]
        heads.append(jnp.concatenate(parts, axis=0))
    yg = ug * jnp.concatenate(heads, axis=1)
    ygn = _rms(yg, ong_ref[...])

    mix = jnp.concatenate([ysn, ygn], axis=1).astype(BF16)
    h1 = x_ref[...] + jnp.dot(mix, wout_ref[...], preferred_element_type=F32)
    h1_ref[...] = h1
    hn_ref[...] = _rms(h1, lnf_ref[...])


def _mix_out(x2, y2, u2, v2, glu_w, glu_b, ln_g, ln_b, ws, bst, ons, ong, w_out, lnf, tm, chunk):
    t, d = x2.shape
    row = lambda i: (i, 0)
    rows = lambda a: pl.BlockSpec((tm, a.shape[-1]), row)
    consts = [glu_w, glu_b, ln_g, ln_b, ws, bst, ons, ong, w_out, lnf]
    return pl.pallas_call(
        functools.partial(_mix_out_kernel, chunk=chunk),
        grid=(t // tm,),
        in_specs=[rows(x2), rows(y2), rows(u2), rows(v2)] + [_const_spec(c.shape) for c in consts],
        out_specs=[pl.BlockSpec((tm, d), row), pl.BlockSpec((tm, d), row)],
        out_shape=[jax.ShapeDtypeStruct((t, d), F32), jax.ShapeDtypeStruct((t, d), F32)],
        compiler_params=pltpu.CompilerParams(dimension_semantics=("parallel",),
                                             vmem_limit_bytes=VMEM_LIMIT),
        name="mix_out",
    )(x2, y2, u2, v2, *consts)


def _topk_rows(sc, k):
    n = sc.shape[0]
    iota = lax.broadcasted_iota(jnp.int32, sc.shape, 0)
    vals, idxs = [], []
    for _ in range(k):
        m = jnp.max(sc, axis=0, keepdims=True)
        i = jnp.min(jnp.where(sc == m, iota, n), axis=0, keepdims=True)
        vals.append(m)
        idxs.append(i)
        sc = jnp.where(iota == i, -jnp.inf, sc)
    return jnp.concatenate(vals, axis=0), jnp.concatenate(idxs, axis=0)


def _router_kernel(hn_ref, wq_ref, keys_ref, e_ref, g_ref, *, topk):
    n_heads, _, n_keys, half = keys_ref.shape
    q = jnp.dot(hn_ref[...].astype(BF16), wq_ref[...], preferred_element_type=F32)
    nt = (((1,), (1,)), ((), ()))
    for h in range(n_heads):
        tops = []
        for k in range(2):
            qhk = q[:, (2 * h + k) * half:(2 * h + k + 1) * half].astype(BF16)
            sct = lax.dot_general(keys_ref[h, k], qhk, nt, preferred_element_type=F32)
            tops.append(_topk_rows(sct, topk))
        (s1, i1), (s2, i2) = tops
        cand = jnp.concatenate([s1[a:a + 1] + s2 for a in range(topk)], axis=0)
        ts, ci = _topk_rows(cand, topk)
        ia = ci // topk
        ib = ci % topk
        e1 = jnp.zeros_like(ci)
        e2 = jnp.zeros_like(ci)
        for a in range(topk):
            e1 = jnp.where(ia == a, i1[a:a + 1], e1)
            e2 = jnp.where(ib == a, i2[a:a + 1], e2)
        p = jnp.exp(ts - ts[0:1])
        rows = slice(h * topk, (h + 1) * topk)
        e_ref[rows, :] = e1 * n_keys + e2
        g_ref[rows, :] = p / jnp.sum(p, axis=0, keepdims=True)


def _router(hn2, wq, keys, topk, tm):
    t, d = hn2.shape
    n_heads = keys.shape[0]
    out_blk = pl.BlockSpec((n_heads * topk, tm), lambda i: (0, i))
    return pl.pallas_call(
        functools.partial(_router_kernel, topk=topk),
        grid=(t // tm,),
        in_specs=[pl.BlockSpec((tm, d), lambda i: (i, 0)), _const_spec(wq.shape),
                  _const_spec(keys.shape)],
        out_specs=[out_blk, out_blk],
        out_shape=[jax.ShapeDtypeStruct((n_heads * topk, t), jnp.int32),
                   jax.ShapeDtypeStruct((n_heads * topk, t), F32)],
        compiler_params=pltpu.CompilerParams(dimension_semantics=("parallel",),
                                             vmem_limit_bytes=VMEM_LIMIT),
        name="router",
    )(hn2, wq, keys)


_BITREV3 = (0, 4, 2, 6, 1, 5, 3, 7)


def _sublane_sums(ps, sub_iota):
    ps = [ps[_BITREV3[i]] for i in range(SUBLANES)]
    d = SUBLANES // 2
    while len(ps) > 1:
        lo = (sub_iota & d) == 0
        nxt = []
        for a in range(0, len(ps), 2):
            x, y = ps[a], ps[a + 1]
            if 2 * d == SUBLANES:
                nxt.append(jnp.where(lo, x, y) + pltpu.roll(jnp.where(lo, y, x), d, 0))
            else:
                nxt.append(jnp.where(lo, x + pltpu.roll(x, SUBLANES - d, 0), y + pltpu.roll(y, d, 0)))
        ps = nxt
        d //= 2
    return ps[0]


def _peer_kernel(idx_ref, hn_ref, g_ref, h1_ref, tab_ref, o_ref, buf0_ref, buf1_ref, act_ref, gt_ref,
                 sem_ref, *, sub):
    tb, n_sel = idx_ref.shape
    n_sub = tb // sub
    bufs = (buf0_ref, buf1_ref)
    sub_iota = lax.broadcasted_iota(jnp.int32, (SUBLANES, LANES), 0)

    def issue_token(row, slot, t):
        for k in range(n_sel):
            pltpu.make_async_copy(tab_ref.at[idx_ref[row, k]], bufs[slot].at[t, k],
                                  sem_ref.at[slot, t]).start()

    def wait_token(slot, t):
        pltpu.make_async_copy(tab_ref.at[pl.ds(0, n_sel)], bufs[slot].at[t], sem_ref.at[slot, t]).wait()

    def compute_token(row, slot, t):
        buf = bufs[slot]
        x = hn_ref[row]
        for j in range(n_sel // SUBLANES):
            rows = slice(j * SUBLANES, (j + 1) * SUBLANES)
            ps = [buf[t, j * SUBLANES + i].astype(F32)[:SUBLANES] * x for i in range(SUBLANES)]
            s = jnp.sum(_sublane_sums(ps, sub_iota), axis=1, keepdims=True)
            act_ref[rows, :] = _gelu(s) * gt_ref[t, rows, :]
        accs = [None] * 4
        for e in range(n_sel):
            a = jnp.broadcast_to(act_ref[e:e + 1, :], (SUBLANES, LANES))
            term = a * buf[t, e].astype(F32)[SUBLANES:]
            accs[e % 4] = term if accs[e % 4] is None else accs[e % 4] + term
        o_ref[row] = h1_ref[row] + ((accs[0] + accs[1]) + (accs[2] + accs[3]))

    def run_sub(sb, slot, prefetch):
        base = pl.multiple_of(sb * sub, sub)
        gt = g_ref[pl.ds(base, sub), :].T
        for t in range(sub):
            gt_ref[t] = jnp.broadcast_to(gt[:, t:t + 1], (n_sel, LANES))

        def body(t, carry):
            wait_token(slot, t)
            if prefetch:
                issue_token(base + sub + t, 1 - slot, t)
            compute_token(base + t, slot, t)
            return carry
        lax.fori_loop(0, sub, body, 0)

    def prologue(t, carry):
        issue_token(t, 0, t)
        return carry
    lax.fori_loop(0, sub, prologue, 0)

    def pair(p, carry):
        run_sub(2 * p, 0, True)
        run_sub(2 * p + 1, 1, True)
        return carry
    lax.fori_loop(0, n_sub // 2 - 1, pair, 0)
    run_sub(n_sub - 2, 0, True)
    run_sub(n_sub - 1, 1, False)


def _peer(experts, hn3, gates, h13, table, tb, sub):
    t = hn3.shape[0]
    n_sel = experts.shape[-1]
    assert (tb // sub) % 2 == 0 and tb % sub == 0
    row2 = lambda i: (i, 0)
    row3 = lambda i: (i, 0, 0)
    tok_blk = pl.BlockSpec((tb,) + hn3.shape[1:], row3)
    tile = table.shape[1:]
    return pl.pallas_call(
        functools.partial(_peer_kernel, sub=sub),
        grid=(t // tb,),
        in_specs=[pl.BlockSpec((tb, n_sel), row2, memory_space=pltpu.SMEM), tok_blk,
                  pl.BlockSpec((tb, n_sel), row2), tok_blk, pl.BlockSpec(memory_space=pl.ANY)],
        out_specs=tok_blk,
        out_shape=jax.ShapeDtypeStruct(hn3.shape, F32),
        scratch_shapes=[pltpu.VMEM((sub, n_sel) + tile, table.dtype),
                        pltpu.VMEM((sub, n_sel) + tile, table.dtype),
                        pltpu.VMEM((n_sel, LANES), F32),
                        pltpu.VMEM((sub, n_sel, LANES), F32),
                        pltpu.SemaphoreType.DMA((2, sub))],
        compiler_params=pltpu.CompilerParams(dimension_semantics=("arbitrary",),
                                             vmem_limit_bytes=VMEM_LIMIT),
        name="peer",
    )(experts, hn3, gates, h13, table)


def _ple_kernel(h_ref, p_ref, ng_ref, wg_ref, wp_ref, fg_ref, o_ref):
    h = h_ref[...]
    hn = _rms(h, ng_ref[...])
    gate = jax.nn.sigmoid(jnp.dot(hn.astype(BF16), wg_ref[...], preferred_element_type=F32))
    proj = jnp.dot(p_ref[...].astype(BF16), wp_ref[...], preferred_element_type=F32)
    o_ref[...] = _rms(h + gate * proj, fg_ref[...])


def _ple(h2, p2, ng, wg, wp, fg, tm):
    t, d = h2.shape
    row = lambda i: (i, 0)
    return pl.pallas_call(
        _ple_kernel,
        grid=(t // tm,),
        in_specs=[pl.BlockSpec((tm, d), row), pl.BlockSpec((tm, p2.shape[-1]), row),
                  _const_spec(ng.shape), _const_spec(wg.shape), _const_spec(wp.shape),
                  _const_spec(fg.shape)],
        out_specs=pl.BlockSpec((tm, d), row),
        out_shape=jax.ShapeDtypeStruct((t, d), F32),
        compiler_params=pltpu.CompilerParams(dimension_semantics=("parallel",),
                                             vmem_limit_bytes=VMEM_LIMIT),
        name="ple_final",
    )(h2, p2, ng, wg, wp, fg)


def _s5_params(a_re, a_im, log_dt, b_re, b_im, c_re, c_im):
    g, p = a_re.shape
    c = b_re.shape[-1]
    dt = jnp.exp(log_dt)[:, None]
    mag = jnp.exp(a_re * dt)
    ang = a_im * dt
    abar_re, abar_im = mag * jnp.cos(ang), mag * jnp.sin(ang)
    nr, ni = abar_re - 1.0, abar_im
    den = a_re * a_re + a_im * a_im
    coef_re = (nr * a_re + ni * a_im) / den
    coef_im = (ni * a_re - nr * a_im) / den
    bbar_re = coef_re[..., None] * b_re - coef_im[..., None] * b_im
    bbar_im = coef_re[..., None] * b_im + coef_im[..., None] * b_re
    eye = jnp.eye(g, dtype=F32)
    blockdiag_in = lambda m: jnp.einsum('gpc,gh->gchp', m, eye).reshape(g * c, g * p)
    blockdiag_out = lambda m: jnp.einsum('gcp,gh->gphc', m, eye).reshape(g * p, g * c)
    bbd = jnp.concatenate([blockdiag_in(bbar_re), blockdiag_in(bbar_im)], axis=1)
    cbd = jnp.concatenate([blockdiag_out(c_re), blockdiag_out(-c_im)], axis=0)
    return (bbd.astype(BF16), cbd.astype(BF16), abar_re.reshape(1, g * p), abar_im.reshape(1, g * p))


def _pack_tables(u_tab, v_tab):
    n, d = u_tab.shape
    tiles = lambda a: a.astype(BF16).reshape(n, d // LANES, LANES)
    return jnp.concatenate([tiles(u_tab), tiles(v_tab)], axis=1)


def _pick(n, pref):
    while n % pref:
        pref //= 2
    return pref


def kernel(x, p, ln_mix_g, w_in, s5_a_re, s5_a_im, s5_log_dt, s5_b_re, s5_b_im, s5_c_re, s5_c_im, s5_d, glu_w, glu_b, sgu_ln_g, sgu_ln_b, sgu_ws, sgu_b, out_norm_ssm, out_norm_sgu, w_out, ln_ffn_g, peer_wq, peer_keys, peer_u, peer_v, ple_norm_g, ple_w_gate, ple_w_proj, final_g):
    bsz, seq, d = x.shape
    t = bsz * seq
    depth = w_in.shape[0]
    d_ssm = glu_w.shape[-1]
    d_sgu = sgu_ln_g.shape[-1]
    chunk = sgu_ws.shape[-1]
    topk = 16
    r1 = lambda a: a.reshape(1, -1)

    assert depth == 1, "the final rmsnorm is fused into the last layer's ple kernel"
    h = x.reshape(t, d)
    for i in range(depth):
        xs, u, v = _mix_in(h, r1(ln_mix_g[i]), w_in[i].astype(BF16), d_ssm, d_sgu, _pick(t, 512))
        bbd, cbd, a_re, a_im = _s5_params(s5_a_re[i], s5_a_im[i], s5_log_dt[i], s5_b_re[i],
                                          s5_b_im[i], s5_c_re[i], s5_c_im[i])
        y = _s5(xs.reshape(bsz, seq, d_ssm), bbd, cbd, a_re, a_im, r1(s5_d[i]),
                _pick(bsz, SUBLANES), _pick(seq, 64))
        h1, hn = _mix_out(h, y.reshape(t, d_ssm), u, v, glu_w[i].astype(BF16), r1(glu_b[i]),
                          r1(sgu_ln_g[i]), r1(sgu_ln_b[i]), sgu_ws[i], sgu_b[i].T,
                          r1(out_norm_ssm[i]), r1(out_norm_sgu[i]), w_out[i].astype(BF16),
                          r1(ln_ffn_g[i]), _pick(t, 2 * chunk), chunk)
        e_t, g_t = _router(hn, peer_wq[i].astype(BF16), peer_keys[i].astype(BF16), topk,
                           _pick(t, 256))
        tok3 = lambda a: a.reshape(t, d // LANES, LANES)
        h2 = _peer(e_t.T, tok3(hn), g_t.T, tok3(h1), _pack_tables(peer_u[i], peer_v[i]),
                   _pick(t, 256), SUBLANES)
        h = _ple(h2.reshape(t, d), p[i].reshape(t, -1), r1(ple_norm_g[i]), ple_w_gate[i].astype(BF16),
                 ple_w_proj[i].astype(BF16), r1(final_g), _pick(t, 512))
    return h.reshape(bsz, seq, d)
```

```python
import functools
import math

import jax
import jax.numpy as jnp
from jax import lax
from jax.experimental import pallas as pl
from jax.experimental.pallas import tpu as pltpu

F32 = jnp.float32
BF16 = jnp.bfloat16
EPS = 1e-6
LANES = 128
SUBLANES = 8
VMEM_LIMIT = 56 * 1024 * 1024
DMA_THREADS = 2


def _gelu(x):
    return 0.5 * x * (1.0 + lax.erf(x * (1.0 / math.sqrt(2.0))))


def _rms(x, g):
    return x * lax.rsqrt(jnp.mean(x * x, axis=-1, keepdims=True) + EPS) * g


def _const_spec(shape):
    nd = len(shape)
    return pl.BlockSpec(shape, lambda *_: (0,) * nd)


def _mix_in_kernel(x_ref, g_ref, w_ref, xs_ref, u_ref, v_ref):
    xn = _rms(x_ref[...], g_ref[...])
    z = jnp.dot(xn.astype(BF16), w_ref[...], preferred_element_type=F32)
    d_ssm = xs_ref.shape[-1]
    d_sgu = u_ref.shape[-1]
    xs_ref[...] = z[:, :d_ssm]
    u_ref[...] = z[:, d_ssm:d_ssm + d_sgu]
    v_ref[...] = z[:, d_ssm + d_sgu:]


def _mix_in(x2, g, w_in, d_ssm, d_sgu, tm):
    t, d = x2.shape
    row = lambda i: (i, 0)
    return pl.pallas_call(
        _mix_in_kernel,
        grid=(t // tm,),
        in_specs=[pl.BlockSpec((tm, d), row), _const_spec((1, d)), _const_spec(w_in.shape)],
        out_specs=[pl.BlockSpec((tm, d_ssm), row), pl.BlockSpec((tm, d_sgu), row),
                   pl.BlockSpec((tm, d_sgu), row)],
        out_shape=[jax.ShapeDtypeStruct((t, d_ssm), F32), jax.ShapeDtypeStruct((t, d_sgu), F32),
                   jax.ShapeDtypeStruct((t, d_sgu), F32)],
        compiler_params=pltpu.CompilerParams(dimension_semantics=("parallel",),
                                             vmem_limit_bytes=VMEM_LIMIT),
        name="mix_in",
    )(x2, g, w_in)


def _s5_kernel(xs_ref, bbd_ref, cbd_ref, are_ref, aim_ref, d_ref, y_ref, bu_ref, hs_ref, h_ref,
               *, seq_blk, n_state, lane_chunk):
    @pl.when(pl.program_id(1) == 0)
    def _():
        h_ref[...] = jnp.zeros_like(h_ref)

    nb = xs_ref.shape[0]
    x = xs_ref[...].reshape(nb * seq_blk, xs_ref.shape[-1])
    bu = jnp.dot(x.astype(BF16), bbd_ref[...], preferred_element_type=F32)
    n_tiles = 2 * n_state // LANES
    for j in range(n_tiles):
        bu_ref[j] = bu[:, j * LANES:(j + 1) * LANES]

    tiles = lane_chunk // LANES
    for c in range(n_state // lane_chunk):
        re = [c * tiles + j for j in range(tiles)]
        im = [n_state // LANES + j for j in re]
        lanes = pl.ds(c * lane_chunk, lane_chunk)
        ar = jnp.broadcast_to(are_ref[:, lanes], (nb, lane_chunk))
        ai = jnp.broadcast_to(aim_ref[:, lanes], (nb, lane_chunk))

        def step(t, carry, re=re, im=im, ar=ar, ai=ai):
            hr, hi = carry
            rows = pl.ds(t, nb, stride=seq_blk)
            br = jnp.concatenate([bu_ref[j, rows, :] for j in re], axis=1)
            bi = jnp.concatenate([bu_ref[j, rows, :] for j in im], axis=1)
            nr = ar * hr - ai * hi + br
            ni = ar * hi + ai * hr + bi
            for k, j in enumerate(re):
                hs_ref[j, rows, :] = nr[:, k * LANES:(k + 1) * LANES]
            for k, j in enumerate(im):
                hs_ref[j, rows, :] = ni[:, k * LANES:(k + 1) * LANES]
            return nr, ni

        im_lanes = pl.ds(n_state + c * lane_chunk, lane_chunk)
        hr, hi = lax.fori_loop(0, seq_blk, step, (h_ref[:, lanes], h_ref[:, im_lanes]))
        h_ref[:, lanes] = hr
        h_ref[:, im_lanes] = hi

    hs = jnp.concatenate([hs_ref[j] for j in range(n_tiles)], axis=1)
    y = jnp.dot(hs.astype(BF16), cbd_ref[...], preferred_element_type=F32)
    y = y + d_ref[...] * x
    y_ref[...] = y.reshape(y_ref.shape)


def _s5(xs3, bbd, cbd, a_re, a_im, d_skip, nb, seq_blk):
    b, s, d_ssm = xs3.shape
    n_state = a_re.shape[-1]
    blk = pl.BlockSpec((nb, seq_blk, d_ssm), lambda i, j: (i, j, 0))
    kern = functools.partial(_s5_kernel, seq_blk=seq_blk, n_state=n_state,
                             lane_chunk=min(n_state, 8 * LANES))
    return pl.pallas_call(
        kern,
        grid=(b // nb, s // seq_blk),
        in_specs=[blk, _const_spec(bbd.shape), _const_spec(cbd.shape), _const_spec(a_re.shape),
                  _const_spec(a_im.shape), _const_spec(d_skip.shape)],
        out_specs=blk,
        out_shape=jax.ShapeDtypeStruct(xs3.shape, F32),
        scratch_shapes=[pltpu.VMEM((2 * n_state // LANES, nb * seq_blk, LANES), F32),
                        pltpu.VMEM((2 * n_state // LANES, nb * seq_blk, LANES), F32),
                        pltpu.VMEM((nb, 2 * n_state), F32)],
        compiler_params=pltpu.CompilerParams(dimension_semantics=("parallel", "arbitrary"),
                                             vmem_limit_bytes=VMEM_LIMIT),
        name="s5_scan",
    )(xs3, bbd, cbd, a_re, a_im, d_skip)


def _mix_out_kernel(x_ref, y_ref, u_ref, v_ref, gluw_ref, glub_ref, lng_ref, lnb_ref, ws_ref,
                    bst_ref, ons_ref, ong_ref, wout_ref, lnf_ref, h1_ref, hn_ref, *, chunk):
    tm = x_ref.shape[0]
    n_heads = ws_ref.shape[0]
    hd = u_ref.shape[-1] // n_heads

    ys = _gelu(y_ref[...])
    gl = jnp.dot(ys.astype(BF16), gluw_ref[...], preferred_element_type=F32) + glub_ref[...]
    ys = ys * jax.nn.sigmoid(gl)
    ysn = _rms(ys, ons_ref[...])

    ug = _gelu(u_ref[...])
    vg = _gelu(v_ref[...])
    row = lax.broadcasted_iota(jnp.int32, (chunk, chunk), 0)
    col = lax.broadcasted_iota(jnp.int32, (chunk, chunk), 1)
    causal = row >= col
    heads = []
    for h in range(n_heads):
        cols = slice(h * hd, (h + 1) * hd)
        vh = vg[:, cols]
        mu = jnp.mean(vh, axis=-1, keepdims=True)
        var = jnp.mean(jnp.square(vh - mu), axis=-1, keepdims=True)
        vn = ((vh - mu) * lax.rsqrt(var + EPS) * lng_ref[:, cols] + lnb_ref[:, cols]).astype(BF16)
        w = jnp.where(causal, ws_ref[h], 0.0).astype(BF16)
        bias = bst_ref[:, h:h + 1]
        parts = [jnp.dot(w, vn[c * chunk:(c + 1) * chunk, :], preferred_element_type=F32) + bias
                 for c in range(tm // chunk)]
        heads.append(jnp.concatenate(parts, axis=0))
    yg = ug * jnp.concatenate(heads, axis=1)
    ygn = _rms(yg, ong_ref[...])

    mix = jnp.concatenate([ysn, ygn], axis=1).astype(BF16)
    h1 = x_ref[...] + jnp.dot(mix, wout_ref[...], preferred_element_type=F32)
    h1_ref[...] = h1
    hn_ref[...] = _rms(h1, lnf_ref[...])


def _mix_out(x2, y2, u2, v2, glu_w, glu_b, ln_g, ln_b, ws, bst, ons, ong, w_out, lnf, tm, chunk):
    t, d = x2.shape
    row = lambda i: (i, 0)
    rows = lambda a: pl.BlockSpec((tm, a.shape[-1]), row)
    consts = [glu_w, glu_b, ln_g, ln_b, ws, bst, ons, ong, w_out, lnf]
    return pl.pallas_call(
        functools.partial(_mix_out_kernel, chunk=chunk),
        grid=(t // tm,),
        in_specs=[rows(x2), rows(y2), rows(u2), rows(v2)] + [_const_spec(c.shape) for c in consts],
        out_specs=[pl.BlockSpec((tm, d), row), pl.BlockSpec((tm, d), row)],
        out_shape=[jax.ShapeDtypeStruct((t, d), F32), jax.ShapeDtypeStruct((t, d), F32)],
        compiler_params=pltpu.CompilerParams(dimension_semantics=("parallel",),
                                             vmem_limit_bytes=VMEM_LIMIT),
        name="mix_out",
    )(x2, y2, u2, v2, *consts)


def _topk_rows(sc, k):
    n = sc.shape[0]
    iota = lax.broadcasted_iota(jnp.int32, sc.shape, 0)
    vals, idxs = [], []
    for _ in range(k):
        m = jnp.max(sc, axis=0, keepdims=True)
        i = jnp.min(jnp.where(sc == m, iota, n), axis=0, keepdims=True)
        vals.append(m)
        idxs.append(i)
        sc = jnp.where(iota == i, -jnp.inf, sc)
    return jnp.concatenate(vals, axis=0), jnp.concatenate(idxs, axis=0)


def _router_kernel(hn_ref, wq_ref, keys_ref, e_ref, g_ref, *, topk):
    n_heads, _, n_keys, half = keys_ref.shape
    q = jnp.dot(hn_ref[...].astype(BF16), wq_ref[...], preferred_element_type=F32)
    nt = (((1,), (1,)), ((), ()))
    for h in range(n_heads):
        tops = []
        for k in range(2):
            qhk = q[:, (2 * h + k) * half:(2 * h + k + 1) * half].astype(BF16)
            sct = lax.dot_general(keys_ref[h, k], qhk, nt, preferred_element_type=F32)
            tops.append(_topk_rows(sct, topk))
        (s1, i1), (s2, i2) = tops
        cand = jnp.concatenate([s1[a:a + 1] + s2 for a in range(topk)], axis=0)
        ts, ci = _topk_rows(cand, topk)
        ia = ci // topk
        ib = ci % topk
        e1 = jnp.zeros_like(ci)
        e2 = jnp.zeros_like(ci)
        for a in range(topk):
            e1 = jnp.where(ia == a, i1[a:a + 1], e1)
            e2 = jnp.where(ib == a, i2[a:a + 1], e2)
        p = jnp.exp(ts - ts[0:1])
        rows = slice(h * topk, (h + 1) * topk)
        e_ref[rows, :] = e1 * n_keys + e2
        g_ref[rows, :] = p / jnp.sum(p, axis=0, keepdims=True)


def _router(hn2, wq, keys, topk, tm):
    t, d = hn2.shape
    n_heads = keys.shape[0]
    out_blk = pl.BlockSpec((n_heads * topk, tm), lambda i: (0, i))
    return pl.pallas_call(
        functools.partial(_router_kernel, topk=topk),
        grid=(t // tm,),
        in_specs=[pl.BlockSpec((tm, d), lambda i: (i, 0)), _const_spec(wq.shape),
                  _const_spec(keys.shape)],
        out_specs=[out_blk, out_blk],
        out_shape=[jax.ShapeDtypeStruct((n_heads * topk, t), jnp.int32),
                   jax.ShapeDtypeStruct((n_heads * topk, t), F32)],
        compiler_params=pltpu.CompilerParams(dimension_semantics=("parallel",),
                                             vmem_limit_bytes=VMEM_LIMIT),
        name="router",
    )(hn2, wq, keys)


_BITREV3 = (0, 4, 2, 6, 1, 5, 3, 7)


def _sublane_sums(ps, sub_iota):
    ps = [ps[_BITREV3[i]] for i in range(SUBLANES)]
    d = SUBLANES // 2
    while len(ps) > 1:
        lo = (sub_iota & d) == 0
        nxt = []
        for a in range(0, len(ps), 2):
            x, y = ps[a], ps[a + 1]
            if 2 * d == SUBLANES:
                nxt.append(jnp.where(lo, x, y) + pltpu.roll(jnp.where(lo, y, x), d, 0))
            else:
                nxt.append(jnp.where(lo, x + pltpu.roll(x, SUBLANES - d, 0), y + pltpu.roll(y, d, 0)))
        ps = nxt
        d //= 2
    return ps[0]


def _peer_kernel(idx_ref, hn_ref, g_ref, h1_ref, tab_ref, o_ref, buf0_ref, buf1_ref, act_ref, gt_ref,
                 sem_ref, *, sub):
    tb, n_sel = idx_ref.shape
    n_sub = tb // sub
    bufs = (buf0_ref, buf1_ref)
    sub_iota = lax.broadcasted_iota(jnp.int32, (SUBLANES, LANES), 0)

    def issue_token(row, slot, t):
        for k in range(n_sel):
            pltpu.make_async_copy(tab_ref.at[idx_ref[row, k]], bufs[slot].at[t, k],
                                  sem_ref.at[slot, t]).start(priority=k % DMA_THREADS)

    def wait_token(slot, t):
        pltpu.make_async_copy(tab_ref.at[pl.ds(0, n_sel)], bufs[slot].at[t], sem_ref.at[slot, t]).wait()

    def compute_token(row, slot, t):
        buf = bufs[slot]
        x = hn_ref[row]
        for j in range(n_sel // SUBLANES):
            rows = slice(j * SUBLANES, (j + 1) * SUBLANES)
            ps = [buf[t, j * SUBLANES + i].astype(F32)[:SUBLANES] * x for i in range(SUBLANES)]
            s = jnp.sum(_sublane_sums(ps, sub_iota), axis=1, keepdims=True)
            act_ref[rows, :] = _gelu(s) * gt_ref[t, rows, :]
        accs = [None] * 4
        for e in range(n_sel):
            a = jnp.broadcast_to(act_ref[e:e + 1, :], (SUBLANES, LANES))
            term = a * buf[t, e].astype(F32)[SUBLANES:]
            accs[e % 4] = term if accs[e % 4] is None else accs[e % 4] + term
        o_ref[row] = h1_ref[row] + ((accs[0] + accs[1]) + (accs[2] + accs[3]))

    def run_sub(sb, slot, prefetch):
        base = pl.multiple_of(sb * sub, sub)
        gt = g_ref[pl.ds(base, sub), :].T
        for t in range(sub):
            gt_ref[t] = jnp.broadcast_to(gt[:, t:t + 1], (n_sel, LANES))

        def body(t, carry):
            wait_token(slot, t)
            if prefetch:
                issue_token(base + sub + t, 1 - slot, t)
            compute_token(base + t, slot, t)
            return carry
        lax.fori_loop(0, sub, body, 0)

    def prologue(t, carry):
        issue_token(t, 0, t)
        return carry
    lax.fori_loop(0, sub, prologue, 0)

    def pair(p, carry):
        run_sub(2 * p, 0, True)
        run_sub(2 * p + 1, 1, True)
        return carry
    lax.fori_loop(0, n_sub // 2 - 1, pair, 0)
    run_sub(n_sub - 2, 0, True)
    run_sub(n_sub - 1, 1, False)


def _peer(experts, hn3, gates, h13, table, tb, sub):
    t = hn3.shape[0]
    n_sel = experts.shape[-1]
    assert (tb // sub) % 2 == 0 and tb % sub == 0
    row2 = lambda i: (i, 0)
    row3 = lambda i: (i, 0, 0)
    tok_blk = pl.BlockSpec((tb,) + hn3.shape[1:], row3)
    tile = table.shape[1:]
    return pl.pallas_call(
        functools.partial(_peer_kernel, sub=sub),
        grid=(t // tb,),
        in_specs=[pl.BlockSpec((tb, n_sel), row2, memory_space=pltpu.SMEM), tok_blk,
                  pl.BlockSpec((tb, n_sel), row2), tok_blk, pl.BlockSpec(memory_space=pl.ANY)],
        out_specs=tok_blk,
        out_shape=jax.ShapeDtypeStruct(hn3.shape, F32),
        scratch_shapes=[pltpu.VMEM((sub, n_sel) + tile, table.dtype),
                        pltpu.VMEM((sub, n_sel) + tile, table.dtype),
                        pltpu.VMEM((n_sel, LANES), F32),
                        pltpu.VMEM((sub, n_sel, LANES), F32),
                        pltpu.SemaphoreType.DMA((2, sub))],
        compiler_params=pltpu.CompilerParams(dimension_semantics=("arbitrary",),
                                             vmem_limit_bytes=VMEM_LIMIT),
        name="peer",
    )(experts, hn3, gates, h13, table)


def _ple_kernel(h_ref, p_ref, ng_ref, wg_ref, wp_ref, fg_ref, o_ref):
    h = h_ref[...]
    hn = _rms(h, ng_ref[...])
    gate = jax.nn.sigmoid(jnp.dot(hn.astype(BF16), wg_ref[...], preferred_element_type=F32))
    proj = jnp.dot(p_ref[...].astype(BF16), wp_ref[...], preferred_element_type=F32)
    o_ref[...] = _rms(h + gate * proj, fg_ref[...])


def _ple(h2, p2, ng, wg, wp, fg, tm):
    t, d = h2.shape
    row = lambda i: (i, 0)
    return pl.pallas_call(
        _ple_kernel,
        grid=(t // tm,),
        in_specs=[pl.BlockSpec((tm, d), row), pl.BlockSpec((tm, p2.shape[-1]), row),
                  _const_spec(ng.shape), _const_spec(wg.shape), _const_spec(wp.shape),
                  _const_spec(fg.shape)],
        out_specs=pl.BlockSpec((tm, d), row),
        out_shape=jax.ShapeDtypeStruct((t, d), F32),
        compiler_params=pltpu.CompilerParams(dimension_semantics=("parallel",),
                                             vmem_limit_bytes=VMEM_LIMIT),
        name="ple_final",
    )(h2, p2, ng, wg, wp, fg)


def _s5_params(a_re, a_im, log_dt, b_re, b_im, c_re, c_im):
    g, p = a_re.shape
    c = b_re.shape[-1]
    dt = jnp.exp(log_dt)[:, None]
    mag = jnp.exp(a_re * dt)
    ang = a_im * dt
    abar_re, abar_im = mag * jnp.cos(ang), mag * jnp.sin(ang)
    nr, ni = abar_re - 1.0, abar_im
    den = a_re * a_re + a_im * a_im
    coef_re = (nr * a_re + ni * a_im) / den
    coef_im = (ni * a_re - nr * a_im) / den
    bbar_re = coef_re[..., None] * b_re - coef_im[..., None] * b_im
    bbar_im = coef_re[..., None] * b_im + coef_im[..., None] * b_re
    eye = jnp.eye(g, dtype=F32)
    blockdiag_in = lambda m: jnp.einsum('gpc,gh->gchp', m, eye).reshape(g * c, g * p)
    blockdiag_out = lambda m: jnp.einsum('gcp,gh->gphc', m, eye).reshape(g * p, g * c)
    bbd = jnp.concatenate([blockdiag_in(bbar_re), blockdiag_in(bbar_im)], axis=1)
    cbd = jnp.concatenate([blockdiag_out(c_re), blockdiag_out(-c_im)], axis=0)
    return (bbd.astype(BF16), cbd.astype(BF16), abar_re.reshape(1, g * p), abar_im.reshape(1, g * p))


def _pack_tables(u_tab, v_tab):
    n, d = u_tab.shape
    tiles = lambda a: a.astype(BF16).reshape(n, d // LANES, LANES)
    return jnp.concatenate([tiles(u_tab), tiles(v_tab)], axis=1)


def _pick(n, pref):
    while n % pref:
        pref //= 2
    return pref


def kernel(x, p, ln_mix_g, w_in, s5_a_re, s5_a_im, s5_log_dt, s5_b_re, s5_b_im, s5_c_re, s5_c_im, s5_d, glu_w, glu_b, sgu_ln_g, sgu_ln_b, sgu_ws, sgu_b, out_norm_ssm, out_norm_sgu, w_out, ln_ffn_g, peer_wq, peer_keys, peer_u, peer_v, ple_norm_g, ple_w_gate, ple_w_proj, final_g):
    bsz, seq, d = x.shape
    t = bsz * seq
    depth = w_in.shape[0]
    d_ssm = glu_w.shape[-1]
    d_sgu = sgu_ln_g.shape[-1]
    chunk = sgu_ws.shape[-1]
    topk = 16
    r1 = lambda a: a.reshape(1, -1)

    assert depth == 1, "the final rmsnorm is fused into the last layer's ple kernel"
    h = x.reshape(t, d)
    for i in range(depth):
        xs, u, v = _mix_in(h, r1(ln_mix_g[i]), w_in[i].astype(BF16), d_ssm, d_sgu, _pick(t, 512))
        bbd, cbd, a_re, a_im = _s5_params(s5_a_re[i], s5_a_im[i], s5_log_dt[i], s5_b_re[i],
                                          s5_b_im[i], s5_c_re[i], s5_c_im[i])
        y = _s5(xs.reshape(bsz, seq, d_ssm), bbd, cbd, a_re, a_im, r1(s5_d[i]),
                _pick(bsz, SUBLANES), _pick(seq, 64))
        h1, hn = _mix_out(h, y.reshape(t, d_ssm), u, v, glu_w[i].astype(BF16), r1(glu_b[i]),
                          r1(sgu_ln_g[i]), r1(sgu_ln_b[i]), sgu_ws[i], sgu_b[i].T,
                          r1(out_norm_ssm[i]), r1(out_norm_sgu[i]), w_out[i].astype(BF16),
                          r1(ln_ffn_g[i]), _pick(t, 2 * chunk), chunk)
        e_t, g_t = _router(hn, peer_wq[i].astype(BF16), peer_keys[i].astype(BF16), topk,
                           _pick(t, 256))
        tok3 = lambda a: a.reshape(t, d // LANES, LANES)
        h2 = _peer(e_t.T, tok3(hn), g_t.T, tok3(h1), _pack_tables(peer_u[i], peer_v[i]),
                   _pick(t, 256), SUBLANES)
        h = _ple(h2.reshape(t, d), p[i].reshape(t, -1), r1(ple_norm_g[i]), ple_w_gate[i].astype(BF16),
                 ple_w_proj[i].astype(BF16), r1(final_g), _pick(t, 512))
    return h.reshape(bsz, seq, d)
```

```python
import functools
import math

import jax
import jax.numpy as jnp
from jax import lax
from jax.experimental import pallas as pl
from jax.experimental.pallas import tpu as pltpu

F32 = jnp.float32
BF16 = jnp.bfloat16
EPS = 1e-6
LANES = 128
SUBLANES = 8
VMEM_LIMIT = 56 * 1024 * 1024
DMA_THREADS = 2


def _gelu(x):
    return 0.5 * x * (1.0 + lax.erf(x * (1.0 / math.sqrt(2.0))))


def _rms(x, g):
    return x * lax.rsqrt(jnp.mean(x * x, axis=-1, keepdims=True) + EPS) * g


def _const_spec(shape):
    nd = len(shape)
    return pl.BlockSpec(shape, lambda *_: (0,) * nd)


def _mix_in_kernel(x_ref, g_ref, w_ref, xs_ref, u_ref, v_ref):
    nb, tm, d = x_ref.shape
    xn = _rms(x_ref[...].reshape(nb * tm, d), g_ref[...])
    z = jnp.dot(xn.astype(BF16), w_ref[...], preferred_element_type=F32)
    d_ssm = xs_ref.shape[-1]
    d_sgu = u_ref.shape[-1]
    for b in range(nb):
        xs_ref[:, b, :] = z[b * tm:(b + 1) * tm, :d_ssm]
    u_ref[...] = z[:, d_ssm:d_ssm + d_sgu].reshape(u_ref.shape)
    v_ref[...] = z[:, d_ssm + d_sgu:].reshape(v_ref.shape)


def _mix_in(x3, g, w_in, d_ssm, d_sgu, nb, tm):
    bsz, s, d = x3.shape
    bm = lambda i, j: (i, j, 0)
    tmaj = lambda i, j: (j, i, 0)
    return pl.pallas_call(
        _mix_in_kernel,
        grid=(bsz // nb, s // tm),
        in_specs=[pl.BlockSpec((nb, tm, d), bm), _const_spec((1, d)), _const_spec(w_in.shape)],
        out_specs=[pl.BlockSpec((tm, nb, d_ssm), tmaj), pl.BlockSpec((nb, tm, d_sgu), bm),
                   pl.BlockSpec((nb, tm, d_sgu), bm)],
        out_shape=[jax.ShapeDtypeStruct((s, bsz, d_ssm), F32),
                   jax.ShapeDtypeStruct((bsz, s, d_sgu), F32),
                   jax.ShapeDtypeStruct((bsz, s, d_sgu), F32)],
        compiler_params=pltpu.CompilerParams(dimension_semantics=("parallel", "parallel"),
                                             vmem_limit_bytes=VMEM_LIMIT),
        name="mix_in",
    )(x3, g, w_in)


def _s5_kernel(xs_ref, bbd_ref, cbd_ref, are_ref, aim_ref, d_ref, y_ref, bu_ref, hs_ref, h_ref,
               *, n_state, lane_chunk):
    @pl.when(pl.program_id(1) == 0)
    def _():
        h_ref[...] = jnp.zeros_like(h_ref)

    seq_blk, nb, d_ssm = xs_ref.shape
    x = xs_ref[...].reshape(seq_blk * nb, d_ssm)
    bu_ref[...] = jnp.dot(x.astype(BF16), bbd_ref[...], preferred_element_type=F32)

    for c in range(n_state // lane_chunk):
        re = pl.ds(c * lane_chunk, lane_chunk)
        im = pl.ds(n_state + c * lane_chunk, lane_chunk)
        ar = jnp.broadcast_to(are_ref[:, re], (nb, lane_chunk))
        ai = jnp.broadcast_to(aim_ref[:, re], (nb, lane_chunk))

        def step(t, carry, re=re, im=im, ar=ar, ai=ai):
            hr, hi = carry
            rows = pl.ds(pl.multiple_of(t * nb, nb), nb)
            nr = ar * hr - ai * hi + bu_ref[rows, re]
            ni = ar * hi + ai * hr + bu_ref[rows, im]
            hs_ref[rows, re] = nr
            hs_ref[rows, im] = ni
            return nr, ni

        hr, hi = lax.fori_loop(0, seq_blk, step, (h_ref[:, re], h_ref[:, im]), unroll=4)
        h_ref[:, re] = hr
        h_ref[:, im] = hi

    y = jnp.dot(hs_ref[...].astype(BF16), cbd_ref[...], preferred_element_type=F32)
    y = y + d_ref[...] * x
    y_ref[...] = y.reshape(y_ref.shape)


def _s5(xs3, bbd, cbd, a_re, a_im, d_skip, nb, seq_blk):
    s, b, d_ssm = xs3.shape
    n_state = a_re.shape[-1]
    blk = pl.BlockSpec((seq_blk, nb, d_ssm), lambda i, j: (j, i, 0))
    kern = functools.partial(_s5_kernel, n_state=n_state, lane_chunk=min(n_state, 8 * LANES))
    return pl.pallas_call(
        kern,
        grid=(b // nb, s // seq_blk),
        in_specs=[blk, _const_spec(bbd.shape), _const_spec(cbd.shape), _const_spec(a_re.shape),
                  _const_spec(a_im.shape), _const_spec(d_skip.shape)],
        out_specs=blk,
        out_shape=jax.ShapeDtypeStruct(xs3.shape, F32),
        scratch_shapes=[pltpu.VMEM((nb * seq_blk, 2 * n_state), F32),
                        pltpu.VMEM((nb * seq_blk, 2 * n_state), F32),
                        pltpu.VMEM((nb, 2 * n_state), F32)],
        compiler_params=pltpu.CompilerParams(dimension_semantics=("parallel", "arbitrary"),
                                             vmem_limit_bytes=VMEM_LIMIT),
        name="s5_scan",
    )(xs3, bbd, cbd, a_re, a_im, d_skip)


def _mix_out_kernel(x_ref, y_ref, u_ref, v_ref, gluw_ref, glub_ref, lng_ref, lnb_ref, ws_ref,
                    bst_ref, ons_ref, ong_ref, wout_ref, lnf_ref, h1_ref, hn_ref, *, chunk):
    tm = x_ref.shape[0]
    n_heads = ws_ref.shape[0]
    hd = u_ref.shape[-1] // n_heads

    ys = _gelu(y_ref[...])
    gl = jnp.dot(ys.astype(BF16), gluw_ref[...], preferred_element_type=F32) + glub_ref[...]
    ys = ys * jax.nn.sigmoid(gl)
    ysn = _rms(ys, ons_ref[...])

    ug = _gelu(u_ref[...])
    vg = _gelu(v_ref[...])
    row = lax.broadcasted_iota(jnp.int32, (chunk, chunk), 0)
    col = lax.broadcasted_iota(jnp.int32, (chunk, chunk), 1)
    causal = row >= col
    heads = []
    for h in range(n_heads):
        cols = slice(h * hd, (h + 1) * hd)
        vh = vg[:, cols]
        mu = jnp.mean(vh, axis=-1, keepdims=True)
        var = jnp.mean(jnp.square(vh - mu), axis=-1, keepdims=True)
        vn = ((vh - mu) * lax.rsqrt(var + EPS) * lng_ref[:, cols] + lnb_ref[:, cols]).astype(BF16)
        w = jnp.where(causal, ws_ref[h], 0.0).astype(BF16)
        bias = bst_ref[:, h:h + 1]
        parts = [jnp.dot(w, vn[c * chunk:(c + 1) * chunk, :], preferred_element_type=F32) + bias
                 for c in range(tm // chunk)]
        heads.append(jnp.concatenate(parts, axis=0))
    yg = ug * jnp.concatenate(heads, axis=1)
    ygn = _rms(yg, ong_ref[...])

    mix = jnp.concatenate([ysn, ygn], axis=1).astype(BF16)
    h1 = x_ref[...] + jnp.dot(mix, wout_ref[...], preferred_element_type=F32)
    h1_ref[...] = h1
    hn_ref[...] = _rms(h1, lnf_ref[...])


def _mix_out(x2, y_tm, u2, v2, glu_w, glu_b, ln_g, ln_b, ws, bst, ons, ong, w_out, lnf, tm, chunk):
    t, d = x2.shape
    seq_blocks = y_tm.shape[0] // tm
    d_ssm = glu_w.shape[0]
    row = lambda i: (i, 0)
    rows = lambda a: pl.BlockSpec((tm, a.shape[-1]), row)
    y_spec = pl.BlockSpec((tm, d_ssm), lambda i: (i % seq_blocks, i // seq_blocks))
    consts = [glu_w, glu_b, ln_g, ln_b, ws, bst, ons, ong, w_out, lnf]
    return pl.pallas_call(
        functools.partial(_mix_out_kernel, chunk=chunk),
        grid=(t // tm,),
        in_specs=[rows(x2), y_spec, rows(u2), rows(v2)] + [_const_spec(c.shape) for c in consts],
        out_specs=[pl.BlockSpec((tm, d), row), pl.BlockSpec((tm, d), row)],
        out_shape=[jax.ShapeDtypeStruct((t, d), F32), jax.ShapeDtypeStruct((t, d), F32)],
        compiler_params=pltpu.CompilerParams(dimension_semantics=("parallel",),
                                             vmem_limit_bytes=VMEM_LIMIT),
        name="mix_out",
    )(x2, y_tm, u2, v2, *consts)


def _topk_rows(sc, labels, k):
    big = jnp.iinfo(jnp.int32).max
    vals, labs = [], []
    for _ in range(k):
        m = jnp.max(sc, axis=0, keepdims=True)
        i = jnp.min(jnp.where(sc == m, labels, big), axis=0, keepdims=True)
        vals.append(m)
        labs.append(i)
        sc = jnp.where(labels == i, -jnp.inf, sc)
    return jnp.concatenate(vals, axis=0), jnp.concatenate(labs, axis=0)


def _pair_candidates(s1, s2):
    k, m = s1.shape
    assert k == 2 * SUBLANES and s2.shape == s1.shape
    sub = lax.broadcasted_iota(jnp.int32, (SUBLANES, m), 0)
    half = sub < SUBLANES // 2
    row = lambda a: jnp.broadcast_to(s1[a:a + 1], (SUBLANES, m))
    lo, hi = s2[:SUBLANES], s2[SUBLANES:]
    lo_twice = jnp.where(half, lo, pltpu.roll(lo, SUBLANES // 2, 0))
    vals = [row(0) + lo, row(0) + hi, row(1) + lo, row(2) + lo, row(3) + lo,
            jnp.where(half, row(4), row(5)) + lo_twice, jnp.where(half, row(6), row(7)) + lo_twice,
            s1[SUBLANES:] + jnp.broadcast_to(s2[0:1], (SUBLANES, m))]
    b_twice = sub & (SUBLANES // 2 - 1)
    labs = [sub, SUBLANES + sub, k + sub, 2 * k + sub, 3 * k + sub,
            jnp.where(half, 4 * k, 5 * k) + b_twice, jnp.where(half, 6 * k, 7 * k) + b_twice,
            (SUBLANES + sub) * k]
    return jnp.concatenate(vals, axis=0), jnp.concatenate(labs, axis=0)


def _router_kernel(hn_ref, wq_ref, keys_ref, e_ref, g_ref, *, topk):
    n_heads, _, n_keys, half = keys_ref.shape
    tm = hn_ref.shape[0]
    q = jnp.dot(hn_ref[...].astype(BF16), wq_ref[...], preferred_element_type=F32)
    nt = (((1,), (1,)), ((), ()))
    key_ids = lax.broadcasted_iota(jnp.int32, (n_keys, tm), 0)
    for h in range(n_heads):
        tops = []
        for k in range(2):
            qhk = q[:, (2 * h + k) * half:(2 * h + k + 1) * half].astype(BF16)
            sct = lax.dot_general(keys_ref[h, k], qhk, nt, preferred_element_type=F32)
            tops.append(_topk_rows(sct, key_ids, topk))
        (s1, i1), (s2, i2) = tops
        ts, ci = _topk_rows(*_pair_candidates(s1, s2), topk)
        ia = ci // topk
        ib = ci % topk
        e1 = jnp.zeros_like(ci)
        e2 = jnp.zeros_like(ci)
        for a in range(topk):
            e1 = jnp.where(ia == a, i1[a:a + 1], e1)
            e2 = jnp.where(ib == a, i2[a:a + 1], e2)
        p = jnp.exp(ts - ts[0:1])
        rows = slice(h * topk, (h + 1) * topk)
        e_ref[rows, :] = e1 * n_keys + e2
        g_ref[rows, :] = p / jnp.sum(p, axis=0, keepdims=True)


def _router(hn2, wq, keys, topk, tm):
    t, d = hn2.shape
    n_heads = keys.shape[0]
    out_blk = pl.BlockSpec((n_heads * topk, tm), lambda i: (0, i))
    return pl.pallas_call(
        functools.partial(_router_kernel, topk=topk),
        grid=(t // tm,),
        in_specs=[pl.BlockSpec((tm, d), lambda i: (i, 0)), _const_spec(wq.shape),
                  _const_spec(keys.shape)],
        out_specs=[out_blk, out_blk],
        out_shape=[jax.ShapeDtypeStruct((n_heads * topk, t), jnp.int32),
                   jax.ShapeDtypeStruct((n_heads * topk, t), F32)],
        compiler_params=pltpu.CompilerParams(dimension_semantics=("parallel",),
                                             vmem_limit_bytes=VMEM_LIMIT),
        name="router",
    )(hn2, wq, keys)


_BITREV3 = (0, 4, 2, 6, 1, 5, 3, 7)


def _sublane_sums(ps, sub_iota):
    ps = [ps[_BITREV3[i]] for i in range(SUBLANES)]
    d = SUBLANES // 2
    while len(ps) > 1:
        lo = (sub_iota & d) == 0
        nxt = []
        for a in range(0, len(ps), 2):
            x, y = ps[a], ps[a + 1]
            if 2 * d == SUBLANES:
                nxt.append(jnp.where(lo, x, y) + pltpu.roll(jnp.where(lo, y, x), d, 0))
            else:
                nxt.append(jnp.where(lo, x + pltpu.roll(x, SUBLANES - d, 0), y + pltpu.roll(y, d, 0)))
        ps = nxt
        d //= 2
    return ps[0]


def _peer_kernel(idx_ref, hn_ref, g_ref, h1_ref, tab_ref, o_ref, buf0_ref, buf1_ref, act_ref, gt_ref,
                 sem_ref, *, sub):
    tb, n_sel = idx_ref.shape
    n_sub = tb // sub
    bufs = (buf0_ref, buf1_ref)
    sub_iota = lax.broadcasted_iota(jnp.int32, (SUBLANES, LANES), 0)

    def issue_token(row, slot, t):
        for k in range(n_sel):
            pltpu.make_async_copy(tab_ref.at[idx_ref[row, k]], bufs[slot].at[t, k],
                                  sem_ref.at[slot, t]).start(priority=k % DMA_THREADS)

    def wait_token(slot, t):
        pltpu.make_async_copy(tab_ref.at[pl.ds(0, n_sel)], bufs[slot].at[t], sem_ref.at[slot, t]).wait()

    def compute_token(row, slot, t):
        buf = bufs[slot]
        x = hn_ref[row]
        for j in range(n_sel // SUBLANES):
            rows = slice(j * SUBLANES, (j + 1) * SUBLANES)
            ps = [buf[t, j * SUBLANES + i].astype(F32)[:SUBLANES] * x for i in range(SUBLANES)]
            s = jnp.sum(_sublane_sums(ps, sub_iota), axis=1, keepdims=True)
            act_ref[rows, :] = _gelu(s) * gt_ref[t, rows, :]
        accs = [None] * 4
        for e in range(n_sel):
            a = jnp.broadcast_to(act_ref[e:e + 1, :], (SUBLANES, LANES))
            term = a * buf[t, e].astype(F32)[SUBLANES:]
            accs[e % 4] = term if accs[e % 4] is None else accs[e % 4] + term
        o_ref[row] = h1_ref[row] + ((accs[0] + accs[1]) + (accs[2] + accs[3]))

    def run_sub(sb, slot, prefetch):
        base = pl.multiple_of(sb * sub, sub)
        gt = g_ref[pl.ds(base, sub), :].T
        for t in range(sub):
            gt_ref[t] = jnp.broadcast_to(gt[:, t:t + 1], (n_sel, LANES))

        def body(t, carry):
            wait_token(slot, t)
            if prefetch:
                issue_token(base + sub + t, 1 - slot, t)
            compute_token(base + t, slot, t)
            return carry
        lax.fori_loop(0, sub, body, 0)

    def prologue(t, carry):
        issue_token(t, 0, t)
        return carry
    lax.fori_loop(0, sub, prologue, 0)

    def pair(p, carry):
        run_sub(2 * p, 0, True)
        run_sub(2 * p + 1, 1, True)
        return carry
    lax.fori_loop(0, n_sub // 2 - 1, pair, 0)
    run_sub(n_sub - 2, 0, True)
    run_sub(n_sub - 1, 1, False)


def _peer(experts, hn3, gates, h13, table, tb, sub):
    t = hn3.shape[0]
    n_sel = experts.shape[-1]
    assert (tb // sub) % 2 == 0 and tb % sub == 0
    row2 = lambda i: (i, 0)
    row3 = lambda i: (i, 0, 0)
    tok_blk = pl.BlockSpec((tb,) + hn3.shape[1:], row3)
    tile = table.shape[1:]
    return pl.pallas_call(
        functools.partial(_peer_kernel, sub=sub),
        grid=(t // tb,),
        in_specs=[pl.BlockSpec((tb, n_sel), row2, memory_space=pltpu.SMEM), tok_blk,
                  pl.BlockSpec((tb, n_sel), row2), tok_blk, pl.BlockSpec(memory_space=pl.ANY)],
        out_specs=tok_blk,
        out_shape=jax.ShapeDtypeStruct(hn3.shape, F32),
        scratch_shapes=[pltpu.VMEM((sub, n_sel) + tile, table.dtype),
                        pltpu.VMEM((sub, n_sel) + tile, table.dtype),
                        pltpu.VMEM((n_sel, LANES), F32),
                        pltpu.VMEM((sub, n_sel, LANES), F32),
                        pltpu.SemaphoreType.DMA((2, sub))],
        compiler_params=pltpu.CompilerParams(dimension_semantics=("arbitrary",),
                                             vmem_limit_bytes=VMEM_LIMIT),
        name="peer",
    )(experts, hn3, gates, h13, table)


def _ple_kernel(h_ref, p_ref, ng_ref, wg_ref, wp_ref, fg_ref, o_ref):
    h = h_ref[...]
    hn = _rms(h, ng_ref[...])
    gate = jax.nn.sigmoid(jnp.dot(hn.astype(BF16), wg_ref[...], preferred_element_type=F32))
    proj = jnp.dot(p_ref[...].astype(BF16), wp_ref[...], preferred_element_type=F32)
    o_ref[...] = _rms(h + gate * proj, fg_ref[...])


def _ple(h2, p2, ng, wg, wp, fg, tm):
    t, d = h2.shape
    row = lambda i: (i, 0)
    return pl.pallas_call(
        _ple_kernel,
        grid=(t // tm,),
        in_specs=[pl.BlockSpec((tm, d), row), pl.BlockSpec((tm, p2.shape[-1]), row),
                  _const_spec(ng.shape), _const_spec(wg.shape), _const_spec(wp.shape),
                  _const_spec(fg.shape)],
        out_specs=pl.BlockSpec((tm, d), row),
        out_shape=jax.ShapeDtypeStruct((t, d), F32),
        compiler_params=pltpu.CompilerParams(dimension_semantics=("parallel",),
                                             vmem_limit_bytes=VMEM_LIMIT),
        name="ple_final",
    )(h2, p2, ng, wg, wp, fg)


def _s5_params(a_re, a_im, log_dt, b_re, b_im, c_re, c_im):
    g, p = a_re.shape
    c = b_re.shape[-1]
    dt = jnp.exp(log_dt)[:, None]
    mag = jnp.exp(a_re * dt)
    ang = a_im * dt
    abar_re, abar_im = mag * jnp.cos(ang), mag * jnp.sin(ang)
    nr, ni = abar_re - 1.0, abar_im
    den = a_re * a_re + a_im * a_im
    coef_re = (nr * a_re + ni * a_im) / den
    coef_im = (ni * a_re - nr * a_im) / den
    bbar_re = coef_re[..., None] * b_re - coef_im[..., None] * b_im
    bbar_im = coef_re[..., None] * b_im + coef_im[..., None] * b_re
    eye = jnp.eye(g, dtype=F32)
    blockdiag_in = lambda m: jnp.einsum('gpc,gh->gchp', m, eye).reshape(g * c, g * p)
    blockdiag_out = lambda m: jnp.einsum('gcp,gh->gphc', m, eye).reshape(g * p, g * c)
    bbd = jnp.concatenate([blockdiag_in(bbar_re), blockdiag_in(bbar_im)], axis=1)
    cbd = jnp.concatenate([blockdiag_out(c_re), blockdiag_out(-c_im)], axis=0)
    return (bbd.astype(BF16), cbd.astype(BF16), abar_re.reshape(1, g * p), abar_im.reshape(1, g * p))


def _pack_tables(u_tab, v_tab):
    n, d = u_tab.shape
    tiles = lambda a: a.astype(BF16).reshape(n, d // LANES, LANES)
    return jnp.concatenate([tiles(u_tab), tiles(v_tab)], axis=1)


def _pick(n, pref):
    while n % pref:
        pref //= 2
    return pref


def kernel(x, p, ln_mix_g, w_in, s5_a_re, s5_a_im, s5_log_dt, s5_b_re, s5_b_im, s5_c_re, s5_c_im, s5_d, glu_w, glu_b, sgu_ln_g, sgu_ln_b, sgu_ws, sgu_b, out_norm_ssm, out_norm_sgu, w_out, ln_ffn_g, peer_wq, peer_keys, peer_u, peer_v, ple_norm_g, ple_w_gate, ple_w_proj, final_g):
    bsz, seq, d = x.shape
    t = bsz * seq
    depth = w_in.shape[0]
    d_ssm = glu_w.shape[-1]
    d_sgu = sgu_ln_g.shape[-1]
    chunk = sgu_ws.shape[-1]
    topk = 16
    r1 = lambda a: a.reshape(1, -1)

    assert depth == 1, "the final rmsnorm is fused into the last layer's ple kernel"
    h = x.reshape(t, d)
    nb = _pick(bsz, SUBLANES)
    for i in range(depth):
        xs, u, v = _mix_in(h.reshape(bsz, seq, d), r1(ln_mix_g[i]), w_in[i].astype(BF16), d_ssm, d_sgu,
                           nb, _pick(seq, 64))
        bbd, cbd, a_re, a_im = _s5_params(s5_a_re[i], s5_a_im[i], s5_log_dt[i], s5_b_re[i],
                                          s5_b_im[i], s5_c_re[i], s5_c_im[i])
        y = _s5(xs, bbd, cbd, a_re, a_im, r1(s5_d[i]), nb, _pick(seq, 64))
        h1, hn = _mix_out(h, y.reshape(seq, bsz * d_ssm), u.reshape(t, d_sgu), v.reshape(t, d_sgu),
                          glu_w[i].astype(BF16), r1(glu_b[i]),
                          r1(sgu_ln_g[i]), r1(sgu_ln_b[i]), sgu_ws[i], sgu_b[i].T,
                          r1(out_norm_ssm[i]), r1(out_norm_sgu[i]), w_out[i].astype(BF16),
                          r1(ln_ffn_g[i]), _pick(t, 2 * chunk), chunk)
        e_t, g_t = _router(hn, peer_wq[i].astype(BF16), peer_keys[i].astype(BF16), topk,
                           _pick(t, LANES))
        tok3 = lambda a: a.reshape(t, d // LANES, LANES)
        h2 = _peer(e_t.T, tok3(hn), g_t.T, tok3(h1), _pack_tables(peer_u[i], peer_v[i]),
                   _pick(t, 256), SUBLANES)
        h = _ple(h2.reshape(t, d), p[i].reshape(t, -1), r1(ple_norm_g[i]), ple_w_gate[i].astype(BF16),
                 ple_w_proj[i].astype(BF16), r1(final_g), _pick(t, 512))
    return h.reshape(bsz, seq, d)
```
